```python
import jax, jax.numpy as jnp
from jax import lax
import numpy as np

D_MODEL = 4096
BATCH = 1
SEQ = 8192
DEPTH = 4

HEAD_DIM = 64
ATTN_HEADS = D_MODEL // 128
ATTN_KV_HEADS = ATTN_HEADS // 8
WINDOW = 128
ATTN_WIDTH = ATTN_HEADS * HEAD_DIM
KV_WIDTH = ATTN_KV_HEADS * HEAD_DIM
SSM_HEAD_DIM = 64
SSM_HEADS = D_MODEL // 128
SSM_GROUPS = 4
SSM_STATE = 128
SSM_CONV = 4
SSM_CHUNK = 128
SSM_WIDTH = SSM_HEADS * SSM_HEAD_DIM
SSM_BC = SSM_GROUPS * SSM_STATE
CONV_CH = SSM_WIDTH + 2 * SSM_BC
IN_WIDTH = ATTN_WIDTH + 2 * KV_WIDTH + SSM_WIDTH + CONV_CH + SSM_HEADS
MIX_WIDTH = ATTN_WIDTH + SSM_WIDTH
GM_WIDTH = D_MODEL
GM_GROUPS = 8
GM_CHUNK = 128
FFN_DIM = D_MODEL * 3 // 2
EPS = 1e-6

kernel_name = 'hybrid_swa_ssd_gmlp_macaron'


def rms_norm(x, g):
    xf = x.astype(jnp.float32)
    y = xf * lax.rsqrt(jnp.mean(xf * xf, axis=-1, keepdims=True) + EPS)
    return (y * g.astype(jnp.float32)).astype(x.dtype)


def layer_norm(x, g, b):
    xf = x.astype(jnp.float32)
    mu = jnp.mean(xf, axis=-1, keepdims=True)
    xc = xf - mu
    y = xc * lax.rsqrt(jnp.mean(xc * xc, axis=-1, keepdims=True) + EPS)
    return (y * g.astype(jnp.float32) + b.astype(jnp.float32)).astype(x.dtype)


def swiglu_ffn(h, w13, w2):
    a, b = jnp.split(h @ w13, 2, axis=-1)
    return (jax.nn.silu(a) * b) @ w2


def sliding_window_attention(q, k, v, q_norm, k_norm, sinks):
    bsz, s = q.shape[0], q.shape[1]
    nb = s // WINDOW
    grp = ATTN_HEADS // ATTN_KV_HEADS
    q = rms_norm(q, q_norm)
    k = rms_norm(k, k_norm)
    qb = q.reshape(bsz, nb, WINDOW, ATTN_KV_HEADS, grp, HEAD_DIM)

    def band(t):
        tp = jnp.pad(t, ((0, 0), (WINDOW, 0), (0, 0), (0, 0)))
        prev = tp[:, :s].reshape(bsz, nb, WINDOW, ATTN_KV_HEADS, HEAD_DIM)
        cur = t.reshape(bsz, nb, WINDOW, ATTN_KV_HEADS, HEAD_DIM)
        return jnp.concatenate([prev, cur], axis=2)

    kb, vb = band(k), band(v)
    scores = jnp.einsum('bnqhgd,bnkhd->bnhgqk', qb, kb).astype(jnp.float32) * (HEAD_DIM ** -0.5)
    qi = jnp.arange(WINDOW)[:, None]
    kj = jnp.arange(2 * WINDOW)[None, :]
    in_band = (kj > qi) & (kj <= qi + WINDOW)
    not_pad = (jnp.arange(nb)[:, None, None] > 0) | (kj >= WINDOW)[None]
    valid = in_band[None] & not_pad
    scores = jnp.where(valid[None, :, None, None], scores, -jnp.inf)
    sink = jnp.broadcast_to(
        sinks.astype(jnp.float32).reshape(ATTN_KV_HEADS, grp)[None, None, :, :, None, None],
        scores.shape[:-1] + (1,))
    probs = jax.nn.softmax(jnp.concatenate([scores, sink], axis=-1), axis=-1)[..., :-1]
    out = jnp.einsum('bnhgqk,bnkhd->bnqhgd', probs.astype(vb.dtype), vb)
    return out.reshape(bsz, s, ATTN_WIDTH)


def causal_depthwise_conv(x, w, b):
    c = x.shape[-1]
    y = lax.conv_general_dilated(x, w[:, None, :], window_strides=(1,),
                                 padding=[(SSM_CONV - 1, 0)],
                                 dimension_numbers=('NWC', 'WIO', 'NWC'),
                                 feature_group_count=c)
    return y + b


def ssd_mixer(z, xbc, dt_raw, conv_w, conv_b, dt_bias, a_log, d_skip, norm_g):
    bsz, s = z.shape[0], z.shape[1]
    nc = s // SSM_CHUNK
    r = SSM_HEADS // SSM_GROUPS
    f32 = jnp.float32
    xbc = jax.nn.silu(causal_depthwise_conv(xbc, conv_w, conv_b))
    xs, bm, cm = jnp.split(xbc, [SSM_WIDTH, SSM_WIDTH + SSM_BC], axis=-1)
    dt = jax.nn.softplus(dt_raw.astype(f32) + dt_bias.astype(f32))
    a = -jnp.exp(a_log.astype(f32))
    x = xs.astype(f32).reshape(bsz, nc, SSM_CHUNK, SSM_GROUPS, r, SSM_HEAD_DIM)
    bc = bm.astype(f32).reshape(bsz, nc, SSM_CHUNK, SSM_GROUPS, SSM_STATE)
    cc = cm.astype(f32).reshape(bsz, nc, SSM_CHUNK, SSM_GROUPS, SSM_STATE)
    dtc = dt.reshape(bsz, nc, SSM_CHUNK, SSM_GROUPS, r)
    a_cs = jnp.cumsum(dtc * a.reshape(SSM_GROUPS, r), axis=2)
    xdt = x * dtc[..., None]
    causal = jnp.tril(jnp.ones((SSM_CHUNK, SSM_CHUNK), dtype=bool))[None, None, :, :, None, None]
    seg = a_cs[:, :, :, None] - a_cs[:, :, None]
    decay = jnp.where(causal, jnp.exp(jnp.where(causal, seg, 0.0)), 0.0)
    cb = jnp.einsum('bclgn,bcsgn->bclsg', cc, bc)
    y_diag = jnp.einsum('bclsgr,bcsgrp->bclgrp', cb[..., None] * decay, xdt)
    decay_to_end = jnp.exp(a_cs[:, :, -1:] - a_cs)
    states = jnp.einsum('bclgn,bclgrp->bcgrpn', bc, xdt * decay_to_end[..., None])
    chunk_decay = jnp.exp(a_cs[:, :, -1])

    def step(h, inp):
        s_c, d_c = inp
        return h * d_c[..., None, None] + s_c, h

    h0 = jnp.zeros((bsz, SSM_GROUPS, r, SSM_HEAD_DIM, SSM_STATE), f32)
    _, prev = lax.scan(step, h0, (jnp.moveaxis(states, 1, 0), jnp.moveaxis(chunk_decay, 1, 0)))
    prev = jnp.moveaxis(prev, 0, 1)
    y_off = jnp.einsum('bclgn,bcgrpn->bclgrp', cc, prev) * jnp.exp(a_cs)[..., None]
    y = y_diag + y_off + x * d_skip.astype(f32).reshape(SSM_GROUPS, r)[:, :, None]
    y = y.reshape(bsz, s, SSM_WIDTH)
    gated = (y * jax.nn.silu(z.astype(f32))).reshape(bsz, s, SSM_GROUPS, SSM_WIDTH // SSM_GROUPS)
    gated = gated * lax.rsqrt(jnp.mean(gated * gated, axis=-1, keepdims=True) + EPS)
    out = gated.reshape(bsz, s, SSM_WIDTH) * norm_g.astype(f32)
    return out.astype(z.dtype)


def hybrid_mixer(h, w_in, q_norm, k_norm, sinks, conv_w, conv_b, dt_bias, a_log, d_skip, ssm_norm, w_out):
    bsz, s = h.shape[0], h.shape[1]
    proj = h @ w_in
    splits = np.cumsum([ATTN_WIDTH, KV_WIDTH, KV_WIDTH, SSM_WIDTH, CONV_CH]).tolist()
    q, k, v, z, xbc, dt_raw = jnp.split(proj, splits, axis=-1)
    attn = sliding_window_attention(
        q.reshape(bsz, s, ATTN_HEADS, HEAD_DIM),
        k.reshape(bsz, s, ATTN_KV_HEADS, HEAD_DIM),
        v.reshape(bsz, s, ATTN_KV_HEADS, HEAD_DIM),
        q_norm, k_norm, sinks)
    ssm = ssd_mixer(z, xbc, dt_raw, conv_w, conv_b, dt_bias, a_log, d_skip, ssm_norm)
    return jnp.concatenate([attn, ssm], axis=-1) @ w_out


def chunked_gmlp(h, w_in, ln_g, ln_b, ws, bs, w_out):
    bsz, s = h.shape[0], h.shape[1]
    nc = s // GM_CHUNK
    u, v = jnp.split(jax.nn.gelu(h @ w_in), 2, axis=-1)
    v = layer_norm(v, ln_g, ln_b)
    vb = v.reshape(bsz, nc, GM_CHUNK, GM_GROUPS, GM_WIDTH // GM_GROUPS)
    w_causal = ws * jnp.tril(jnp.ones((GM_CHUNK, GM_CHUNK), dtype=ws.dtype))
    sv = jnp.einsum('gij,bcjgd->bcigd', w_causal, vb) + bs.T[None, None, :, :, None]
    return (u * sv.reshape(bsz, s, GM_WIDTH)) @ w_out


def setup_inputs(seed: int = 0) -> dict:
    key = jax.random.key(seed)
    ks = jax.random.split(key, 24)
    n_even = (DEPTH + 1) // 2
    n_odd = DEPTH // 2
    f32 = jnp.float32

    def normal(k, shape, scale):
        return jax.random.normal(k, shape, f32) * scale

    def gain(k, shape):
        return 1.0 + 0.02 * jax.random.normal(k, shape, f32)

    dt0 = jnp.exp(jax.random.uniform(ks[11], (n_even, SSM_HEADS), f32, np.log(1e-3), np.log(1e-1)))
    return {
        'x': jax.random.normal(ks[0], (BATCH, SEQ, D_MODEL), f32),
        'ffn_norm': gain(ks[1], (DEPTH, 2, D_MODEL)),
        'ffn_w13': normal(ks[2], (DEPTH, 2, D_MODEL, 2 * FFN_DIM), D_MODEL ** -0.5),
        'ffn_w2': normal(ks[3], (DEPTH, 2, FFN_DIM, D_MODEL), FFN_DIM ** -0.5),
        'mix_norm': gain(ks[4], (DEPTH, D_MODEL)),
        'hy_w_in': normal(ks[5], (n_even, D_MODEL, IN_WIDTH), D_MODEL ** -0.5),
        'attn_q_norm': gain(ks[6], (n_even, HEAD_DIM)),
        'attn_k_norm': gain(ks[7], (n_even, HEAD_DIM)),
        'attn_sinks': normal(ks[8], (n_even, ATTN_HEADS), 0.5),
        'ssm_conv_w': normal(ks[9], (n_even, SSM_CONV, CONV_CH), SSM_CONV ** -0.5),
        'ssm_conv_b': normal(ks[10], (n_even, CONV_CH), 0.02),
        'ssm_dt_bias': dt0 + jnp.log(-jnp.expm1(-dt0)),
        'ssm_A_log': jnp.log(jax.random.uniform(ks[12], (n_even, SSM_HEADS), f32, 1.0, 16.0)),
        'ssm_D': 1.0 + 0.1 * jax.random.normal(ks[13], (n_even, SSM_HEADS), f32),
        'ssm_norm': gain(ks[14], (n_even, SSM_WIDTH)),
        'hy_w_out': normal(ks[15], (n_even, MIX_WIDTH, D_MODEL), MIX_WIDTH ** -0.5),
        'gm_w_in': normal(ks[16], (n_odd, D_MODEL, 2 * GM_WIDTH), D_MODEL ** -0.5),
        'gm_ln_g': gain(ks[17], (n_odd, GM_WIDTH)),
        'gm_ln_b': normal(ks[18], (n_odd, GM_WIDTH), 0.02),
        'gm_ws': normal(ks[19], (n_odd, GM_GROUPS, GM_CHUNK, GM_CHUNK), 0.5 * GM_CHUNK ** -0.5),
        'gm_bs': 1.0 + 0.1 * jax.random.normal(ks[20], (n_odd, GM_GROUPS, GM_CHUNK), f32),
        'gm_w_out': normal(ks[21], (n_odd, GM_WIDTH, D_MODEL), GM_WIDTH ** -0.5),
    }


def reference(x, ffn_norm, ffn_w13, ffn_w2, mix_norm, hy_w_in, attn_q_norm, attn_k_norm, attn_sinks,
              ssm_conv_w, ssm_conv_b, ssm_dt_bias, ssm_A_log, ssm_D, ssm_norm, hy_w_out,
              gm_w_in, gm_ln_g, gm_ln_b, gm_ws, gm_bs, gm_w_out):
    for l in range(DEPTH):
        x = x + 0.5 * swiglu_ffn(rms_norm(x, ffn_norm[l, 0]), ffn_w13[l, 0], ffn_w2[l, 0])
        h = rms_norm(x, mix_norm[l])
        i = l // 2
        if l % 2 == 0:
            x = x + hybrid_mixer(h, hy_w_in[i], attn_q_norm[i], attn_k_norm[i], attn_sinks[i],
                                 ssm_conv_w[i], ssm_conv_b[i], ssm_dt_bias[i], ssm_A_log[i], ssm_D[i],
                                 ssm_norm[i], hy_w_out[i])
        else:
            x = x + chunked_gmlp(h, gm_w_in[i], gm_ln_g[i], gm_ln_b[i], gm_ws[i], gm_bs[i], gm_w_out[i])
        x = x + 0.5 * swiglu_ffn(rms_norm(x, ffn_norm[l, 1]), ffn_w13[l, 1], ffn_w2[l, 1])
    return x
```

```python
import functools

import jax
import jax.numpy as jnp
from jax import lax
from jax.experimental import pallas as pl
from jax.experimental.pallas import tpu as pltpu

F32 = jnp.float32
BF16 = jnp.bfloat16

EPS = 1e-6
LANES = 128
HEAD_DIM = 64
WINDOW = 128
KV_GROUP = 8
SSM_GROUPS = 4
SSM_STATE = 128
SSM_CONV = 4
CHUNK = 128
GM_GROUPS = 8
NEG_BIG = -1e30
VMEM_LIMIT = 56 * 1024 * 1024


def _cparams(n_axes):
    return pltpu.CompilerParams(dimension_semantics=("arbitrary",) * n_axes,
                                vmem_limit_bytes=VMEM_LIMIT)


def _lane_tile(v, reps):
    return v if reps == 1 else jnp.concatenate([v] * reps, axis=1)


def _dot(a, b):
    return jnp.dot(a, b, preferred_element_type=F32)


def _dot_nt(a, b):
    return lax.dot_general(a, b, (((1,), (1,)), ((), ())), preferred_element_type=F32)


def _split2(v):
    hi = v.astype(BF16)
    lo = (v - hi.astype(F32)).astype(BF16)
    return hi, lo


def _split3(v):
    hi = v.astype(BF16)
    r = v - hi.astype(F32)
    mid = r.astype(BF16)
    lo = (r - mid.astype(F32)).astype(BF16)
    return hi, mid, lo


def _sigmoid(v):
    return 1.0 / (1.0 + jnp.exp(-v))


def _silu(v):
    return v * _sigmoid(v)


def _gelu_tanh(v):
    c = 0.7978845608028654
    return 0.5 * v * (1.0 + jnp.tanh(c * (v + 0.044715 * (v * v * v))))


def _row_sumsq_lanes(v):
    n = v.shape[1] // LANES
    acc = v[:, :LANES] * v[:, :LANES]
    for t in range(1, n):
        s = v[:, t * LANES:(t + 1) * LANES]
        acc = acc + s * s
    return acc


def _row_sum_lanes(v):
    n = v.shape[1] // LANES
    acc = v[:, :LANES]
    for t in range(1, n):
        acc = acc + v[:, t * LANES:(t + 1) * LANES]
    return acc


def _prep_kernel(x_ref, g_ref, hb_ref, rstd_ref, *, d_model):
    x = x_ref[...]
    hb_ref[...] = (x * g_ref[...]).astype(BF16)
    ssq = jnp.sum(x * x, axis=-1, keepdims=True)
    rstd_ref[...] = jnp.broadcast_to(lax.rsqrt(ssq / d_model + EPS), rstd_ref.shape)


def _prep(x, gain):
    m, d = x.shape
    bm = min(256, m)
    return pl.pallas_call(
        functools.partial(_prep_kernel, d_model=d),
        grid=(m // bm,),
        in_specs=[pl.BlockSpec((bm, d), lambda i: (i, 0)),
                  pl.BlockSpec((1, d), lambda i: (0, 0))],
        out_specs=[pl.BlockSpec((bm, d), lambda i: (i, 0)),
                   pl.BlockSpec((bm, LANES), lambda i: (i, 0))],
        out_shape=[jax.ShapeDtypeStruct((m, d), BF16),
                   jax.ShapeDtypeStruct((m, LANES), F32)],
        compiler_params=_cparams(1),
        name="prep_norm",
    )(x, gain.reshape(1, d))


def _swiglu_kernel(hb_ref, rstd_ref, wa_ref, wb_ref, o_ref):
    hb = hb_ref[...]
    r = _lane_tile(rstd_ref[...], o_ref.shape[1] // LANES)
    a = _dot(hb, wa_ref[...]) * r
    b = _dot(hb, wb_ref[...]) * r
    o_ref[...] = (_silu(a) * b).astype(o_ref.dtype)


def _ffn_up(hb, rstd, w13, *, bm, bn):
    m, d = hb.shape
    f = w13.shape[1] // 2
    nj = f // bn
    return pl.pallas_call(
        _swiglu_kernel,
        grid=(m // bm, nj),
        in_specs=[pl.BlockSpec((bm, d), lambda i, j: (i, 0)),
                  pl.BlockSpec((bm, LANES), lambda i, j: (i, 0)),
                  pl.BlockSpec((d, bn), lambda i, j: (0, j)),
                  pl.BlockSpec((d, bn), lambda i, j: (0, j + nj))],
        out_specs=pl.BlockSpec((bm, bn), lambda i, j: (i, j)),
        out_shape=jax.ShapeDtypeStruct((m, f), BF16),
        compiler_params=_cparams(2),
        name="ffn_up",
    )(hb, rstd, w13, w13)


def _gmlp_in_kernel(hb_ref, rstd_ref, wu_ref, wv_ref, u_ref, v_ref, mu_ref, rs_ref, *, width):
    j = pl.program_id(1)
    hb = hb_ref[...]
    r = _lane_tile(rstd_ref[...], u_ref.shape[1] // LANES)
    u_ref[...] = _gelu_tanh(_dot(hb, wu_ref[...]) * r).astype(u_ref.dtype)
    v = _gelu_tanh(_dot(hb, wv_ref[...]) * r)
    v_ref[...] = v.astype(v_ref.dtype)
    s1 = _row_sum_lanes(v)
    s2 = _row_sumsq_lanes(v)

    @pl.when(j == 0)
    def _():
        mu_ref[...] = s1
        rs_ref[...] = s2

    @pl.when(j > 0)
    def _():
        mu_ref[...] += s1
        rs_ref[...] += s2

    @pl.when(j == pl.num_programs(1) - 1)
    def _():
        mean = jnp.sum(mu_ref[...], axis=-1, keepdims=True) / width
        ex2 = jnp.sum(rs_ref[...], axis=-1, keepdims=True) / width
        var = ex2 - mean * mean
        mu_ref[...] = jnp.broadcast_to(mean, mu_ref.shape)
        rs_ref[...] = jnp.broadcast_to(lax.rsqrt(var + EPS), rs_ref.shape)


def _gmlp_in(hb, rstd, w_in, *, bm, bn):
    m, d = hb.shape
    width = w_in.shape[1] // 2
    nj = width // bn
    return pl.pallas_call(
        functools.partial(_gmlp_in_kernel, width=width),
        grid=(m // bm, nj),
        in_specs=[pl.BlockSpec((bm, d), lambda i, j: (i, 0)),
                  pl.BlockSpec((bm, LANES), lambda i, j: (i, 0)),
                  pl.BlockSpec((d, bn), lambda i, j: (0, j)),
                  pl.BlockSpec((d, bn), lambda i, j: (0, j + nj))],
        out_specs=[pl.BlockSpec((bm, bn), lambda i, j: (i, j)),
                   pl.BlockSpec((bm, bn), lambda i, j: (i, j)),
                   pl.BlockSpec((bm, LANES), lambda i, j: (i, 0)),
                   pl.BlockSpec((bm, LANES), lambda i, j: (i, 0))],
        out_shape=[jax.ShapeDtypeStruct((m, width), BF16),
                   jax.ShapeDtypeStruct((m, width), BF16),
                   jax.ShapeDtypeStruct((m, LANES), F32),
                   jax.ShapeDtypeStruct((m, LANES), F32)],
        compiler_params=_cparams(2),
        name="gmlp_in",
    )(hb, rstd, w_in, w_in)


def _mixer_in_kernel(hb_ref, rstd_ref, w_ref, wdt_ref, proj_ref, dt_ref):
    hb = hb_ref[...]
    r1 = rstd_ref[...]
    proj_ref[...] = (_dot(hb, w_ref[...]) * _lane_tile(r1, proj_ref.shape[1] // LANES)
                     ).astype(proj_ref.dtype)

    @pl.when(pl.program_id(1) == 0)
    def _():
        dt_ref[...] = _dot(hb, wdt_ref[...]) * r1


def _mixer_in(hb, rstd, w_perm, w_dt, *, bm, bn):
    m, d = hb.shape
    n = w_perm.shape[1]
    return pl.pallas_call(
        _mixer_in_kernel,
        grid=(m // bm, n // bn),
        in_specs=[pl.BlockSpec((bm, d), lambda i, j: (i, 0)),
                  pl.BlockSpec((bm, LANES), lambda i, j: (i, 0)),
                  pl.BlockSpec((d, bn), lambda i, j: (0, j)),
                  pl.BlockSpec((d, LANES), lambda i, j: (0, 0))],
        out_specs=[pl.BlockSpec((bm, bn), lambda i, j: (i, j)),
                   pl.BlockSpec((bm, LANES), lambda i, j: (i, 0))],
        out_shape=[jax.ShapeDtypeStruct((m, n), BF16),
                   jax.ShapeDtypeStruct((m, LANES), F32)],
        compiler_params=_cparams(2),
        name="mixer_in",
    )(hb, rstd, w_perm, w_dt)


def _residual_kernel(*refs, n_a, alpha, with_norm, d_model):
    a_refs = refs[:n_a]
    w_refs = refs[n_a:2 * n_a]
    x_ref = refs[2 * n_a]
    if with_norm:
        g_ref, xo_ref, hb_ref, rstd_ref = refs[2 * n_a + 1:]
    else:
        (xo_ref,) = refs[2 * n_a + 1:]
    acc = _dot(a_refs[0][...], w_refs[0][...])
    for k in range(1, n_a):
        acc = acc + _dot(a_refs[k][...], w_refs[k][...])
    xn = x_ref[...] + (acc if alpha == 1.0 else alpha * acc)
    xo_ref[...] = xn
    if not with_norm:
        return
    hb_ref[...] = (xn * g_ref[...]).astype(hb_ref.dtype)
    part = _row_sumsq_lanes(xn)
    j = pl.program_id(1)

    @pl.when(j == 0)
    def _():
        rstd_ref[...] = part

    @pl.when(j > 0)
    def _():
        rstd_ref[...] += part

    @pl.when(j == pl.num_programs(1) - 1)
    def _():
        ssq = jnp.sum(rstd_ref[...], axis=-1, keepdims=True)
        rstd_ref[...] = jnp.broadcast_to(lax.rsqrt(ssq / d_model + EPS), rstd_ref.shape)


def _residual_proj(a_list, w, x, gain_next, *, alpha, bm, bn):
    m, d = x.shape
    n_a = len(a_list)
    kk = a_list[0].shape[1]
    assert all(a.shape[1] == kk for a in a_list) and kk * n_a == w.shape[0]
    with_norm = gain_next is not None
    in_specs = [pl.BlockSpec((bm, kk), lambda i, j: (i, 0)) for _ in range(n_a)]
    in_specs += [pl.BlockSpec((kk, bn), functools.partial(lambda k, i, j: (k, j), k)) for k in range(n_a)]
    in_specs += [pl.BlockSpec((bm, bn), lambda i, j: (i, j))]
    operands = list(a_list) + [w] * n_a + [x]
    out_specs = [pl.BlockSpec((bm, bn), lambda i, j: (i, j))]
    out_shape = [jax.ShapeDtypeStruct((m, d), F32)]
    if with_norm:
        in_specs += [pl.BlockSpec((1, bn), lambda i, j: (0, j))]
        operands += [gain_next.reshape(1, d)]
        out_specs += [pl.BlockSpec((bm, bn), lambda i, j: (i, j)),
                      pl.BlockSpec((bm, LANES), lambda i, j: (i, 0))]
        out_shape += [jax.ShapeDtypeStruct((m, d), BF16),
                      jax.ShapeDtypeStruct((m, LANES), F32)]
    outs = pl.pallas_call(
        functools.partial(_residual_kernel, n_a=n_a, alpha=alpha, with_norm=with_norm, d_model=d),
        grid=(m // bm, d // bn),
        in_specs=in_specs,
        out_specs=out_specs,
        out_shape=out_shape,
        compiler_params=_cparams(2),
        name="residual_proj",
    )(*operands)
    if with_norm:
        return outs[0], outs[1], outs[2]
    return outs[0], None, None


def _head_ones():
    r = lax.broadcasted_iota(jnp.int32, (LANES, LANES), 0) // HEAD_DIM
    c = lax.broadcasted_iota(jnp.int32, (LANES, LANES), 1) // HEAD_DIM
    return (r == c).astype(BF16)


def _head_rms(v, gain_row):
    rows, width = v.shape
    nt = width // LANES
    ones = _head_ones()
    sq = v * v
    stacked = jnp.concatenate([sq[:, t * LANES:(t + 1) * LANES] for t in range(nt)], axis=0)
    hi, lo = _split2(stacked)
    ssq = _dot(hi, ones) + _dot(lo, ones)
    rs = lax.rsqrt(ssq / HEAD_DIM + EPS)
    outs = [v[:, t * LANES:(t + 1) * LANES] * rs[t * rows:(t + 1) * rows] * gain_row for t in range(nt)]
    return outs


def _attn_kernel(q_ref, kp_ref, kc_ref, vp_ref, vc_ref, qg_ref, kg_ref, sink_ref, o_ref):
    n = pl.program_id(0)
    w = WINDOW
    n_kv = kc_ref.shape[1] // HEAD_DIM
    lane = lax.broadcasted_iota(jnp.int32, (2 * w, LANES), 1)
    low_half = lane < HEAD_DIM

    q_tiles = _head_rms(q_ref[...].astype(F32), qg_ref[...] * (HEAD_DIM ** -0.5))
    kk = jnp.concatenate([kp_ref[...], kc_ref[...]], axis=0).astype(F32)
    k_tiles = _head_rms(kk, kg_ref[...])
    vv = jnp.concatenate([vp_ref[...], vc_ref[...]], axis=0).astype(F32)

    qi = lax.broadcasted_iota(jnp.int32, (w, 4 * w), 0)
    kj = lax.broadcasted_iota(jnp.int32, (w, 4 * w), 1) % (2 * w)
    first_key = jnp.where(n > 0, 0, w)
    valid = (kj > qi) & (kj <= qi + w) & (kj >= first_key)
    first_head = lax.broadcasted_iota(jnp.int32, (w, LANES), 1) < HEAD_DIM

    for h in range(n_kv):
        kt = k_tiles[h // 2]
        vt = vv[:, (h // 2) * LANES:(h // 2 + 1) * LANES]
        if h % 2 == 0:
            k_lo = jnp.where(low_half, kt, 0.0)
            v_lo = jnp.where(low_half, vt, 0.0)
            k_hi = pltpu.roll(k_lo, HEAD_DIM, 1)
            v_hi = pltpu.roll(v_lo, HEAD_DIM, 1)
        else:
            k_hi = jnp.where(low_half, 0.0, kt)
            v_hi = jnp.where(low_half, 0.0, vt)
            k_lo = pltpu.roll(k_hi, HEAD_DIM, 1)
            v_lo = pltpu.roll(v_hi, HEAD_DIM, 1)
        k_bd = jnp.concatenate([k_lo, k_hi], axis=0).astype(BF16)
        v_bd = jnp.concatenate([v_lo, v_hi], axis=0).astype(BF16)
        for p in range(KV_GROUP // 2):
            t = h * (KV_GROUP // 2) + p
            s = _dot_nt(q_tiles[t].astype(BF16), k_bd)
            s = jnp.where(valid, s, NEG_BIG)
            sink = sink_ref[:, t * LANES:(t + 1) * LANES]
            s0, s1 = s[:, :2 * w], s[:, 2 * w:]
            sk0, sk1 = sink[:, :1], sink[:, HEAD_DIM:HEAD_DIM + 1]
            m0 = jnp.maximum(jnp.max(s0, axis=-1, keepdims=True), sk0)
            m1 = jnp.maximum(jnp.max(s1, axis=-1, keepdims=True), sk1)
            p0 = jnp.exp(s0 - m0)
            p1 = jnp.exp(s1 - m1)
            d0 = jnp.sum(p0, axis=-1, keepdims=True) + jnp.exp(sk0 - m0)
            d1 = jnp.sum(p1, axis=-1, keepdims=True) + jnp.exp(sk1 - m1)
            pv = _dot(jnp.concatenate([p0, p1], axis=1).astype(BF16), v_bd)
            inv = jnp.where(first_head, 1.0 / d0, 1.0 / d1)
            o_ref[:, t * LANES:(t + 1) * LANES] = (pv * inv).astype(o_ref.dtype)


def _attention(proj, q_gain, k_gain, sinks, *, q_width, k_col, v_col, kv_width):
    s_len = proj.shape[0]
    nb = s_len // WINDOW
    kb, vb = k_col // kv_width, v_col // kv_width
    gain2 = lambda g: jnp.concatenate([g, g]).reshape(1, LANES).astype(F32)
    sink_lanes = jnp.repeat(sinks.astype(F32), HEAD_DIM).reshape(1, q_width)
    prev = lambda n: jnp.maximum(n - 1, 0)
    return pl.pallas_call(
        _attn_kernel,
        grid=(nb,),
        in_specs=[pl.BlockSpec((WINDOW, q_width), lambda n: (n, 0)),
                  pl.BlockSpec((WINDOW, kv_width), lambda n: (prev(n), kb)),
                  pl.BlockSpec((WINDOW, kv_width), lambda n: (n, kb)),
                  pl.BlockSpec((WINDOW, kv_width), lambda n: (prev(n), vb)),
                  pl.BlockSpec((WINDOW, kv_width), lambda n: (n, vb)),
                  pl.BlockSpec((1, LANES), lambda n: (0, 0)),
                  pl.BlockSpec((1, LANES), lambda n: (0, 0)),
                  pl.BlockSpec((1, q_width), lambda n: (0, 0))],
        out_specs=pl.BlockSpec((WINDOW, q_width), lambda n: (n, 0)),
        out_shape=jax.ShapeDtypeStruct((s_len, q_width), BF16),
        compiler_params=_cparams(1),
        name="swa_attention",
    )(proj, proj, proj, proj, proj, gain2(q_gain), gain2(k_gain), sink_lanes)


def _ssd_kernel(z_ref, xs_ref, b_ref, c_ref, dt_ref, cwx_ref, cwb_ref, cwc_ref, cbx_ref, cbb_ref, cbc_ref,
                dtb_ref, a_ref, dsk_ref, ng_ref, o_ref,
                px_ref, pb_ref, pc_ref, state_ref):
    c_idx = pl.program_id(0)
    L = CHUNK
    hp = xs_ref.shape[1]
    n_heads = hp // HEAD_DIM
    gw = hp // SSM_GROUPS
    hist = 8

    @pl.when(c_idx == 0)
    def _():
        px_ref[0:hist, :] = jnp.zeros((hist, px_ref.shape[1]), F32)
        pb_ref[0:hist, :] = jnp.zeros((hist, pb_ref.shape[1]), F32)
        pc_ref[0:hist, :] = jnp.zeros((hist, pc_ref.shape[1]), F32)
        state_ref[...] = jnp.zeros(state_ref.shape, F32)

    def conv_silu(src_ref, pad_ref, w_ref, bias_ref):
        pad_ref[hist:hist + L, :] = src_ref[...].astype(F32)
        acc = bias_ref[...]
        for k in range(SSM_CONV):
            acc = acc + pad_ref[pl.ds(hist - (SSM_CONV - 1) + k, L), :] * w_ref[k:k + 1, :]
        pad_ref[0:hist, :] = pad_ref[L:L + hist, :]
        return _silu(acc)

    xs = conv_silu(xs_ref, px_ref, cwx_ref, cbx_ref)
    bm = conv_silu(b_ref, pb_ref, cwb_ref, cbb_ref)
    cm = conv_silu(c_ref, pc_ref, cwc_ref, cbc_ref)

    dt_in = dt_ref[...] + dtb_ref[...]
    dt = jnp.maximum(dt_in, 0.0) + jnp.log1p(jnp.exp(-jnp.abs(dt_in)))
    da = dt * a_ref[...]
    ri = lax.broadcasted_iota(jnp.int32, (L, L), 0)
    ci = lax.broadcasted_iota(jnp.int32, (L, L), 1)
    causal = ci <= ri
    tril = causal.astype(BF16)
    p0, p1, p2 = _split3(da)
    a_cs = _dot(tril, p0) + _dot(tril, p1) + _dot(tril, p2)
    a_cs_t = a_cs.T
    a_last = a_cs[L - 1:L, :]
    decay_end = jnp.exp(a_last - a_cs)
    e_acs = jnp.exp(a_cs)

    er = lax.broadcasted_iota(jnp.int32, (LANES, hp), 0)
    ec = lax.broadcasted_iota(jnp.int32, (LANES, hp), 1) // HEAD_DIM
    expand = (er == ec).astype(BF16)

    def expand_heads(v):
        hi, lo = _split2(v)
        return _dot(hi, expand) + _dot(lo, expand)

    dt_e = expand_heads(dt)
    xdt = xs * dt_e
    x_end = (xdt * expand_heads(decay_end)).astype(BF16)
    e_acs_e = expand_heads(e_acs)
    chunk_decay_e = e_acs_e[L - 1:L, :]
    xdt_b = xdt.astype(BF16)

    lane2 = lax.broadcasted_iota(jnp.int32, (L, LANES), 1)
    low_half = lane2 < HEAD_DIM
    y_tiles = []
    for g in range(SSM_GROUPS):
        bg = bm[:, g * SSM_STATE:(g + 1) * SSM_STATE]
        cg = cm[:, g * SSM_STATE:(g + 1) * SSM_STATE].astype(BF16)
        cb = _dot_nt(cg, bg.astype(BF16))
        heads_per_group = n_heads // SSM_GROUPS
        for pr in range(heads_per_group // 2):
            h0 = g * heads_per_group + 2 * pr
            t = h0 // 2
            ms = []
            for h in (h0, h0 + 1):
                seg = a_cs[:, h:h + 1] - a_cs_t[h:h + 1, :]
                ms.append((cb * jnp.exp(jnp.where(causal, seg, NEG_BIG))).astype(BF16))
            xt = xdt_b[:, t * LANES:(t + 1) * LANES]
            zero = jnp.zeros_like(xt)
            rhs = jnp.concatenate([jnp.where(low_half, xt, zero), jnp.where(low_half, zero, xt)], axis=0)
            y_tiles.append(_dot(jnp.concatenate(ms, axis=1), rhs))
    y_diag = jnp.concatenate(y_tiles, axis=1)

    y_off_tiles, new_state = [], []
    for g in range(SSM_GROUPS):
        sl = slice(g * gw, (g + 1) * gw)
        st = state_ref[:, sl]
        cg = cm[:, g * SSM_STATE:(g + 1) * SSM_STATE].astype(BF16)
        y_off_tiles.append(_dot(cg, st.astype(BF16)))
        bg_t = bm[:, g * SSM_STATE:(g + 1) * SSM_STATE].T.astype(BF16)
        new_state.append(st * chunk_decay_e[:, sl] + _dot(bg_t, x_end[:, sl]))
    y_off = jnp.concatenate(y_off_tiles, axis=1) * e_acs_e
    for g in range(SSM_GROUPS):
        state_ref[:, g * gw:(g + 1) * gw] = new_state[g]

    y = y_diag + y_off + xs * dsk_ref[...]
    gated = y * _silu(z_ref[...].astype(F32))
    outs = []
    for g in range(SSM_GROUPS):
        gg = gated[:, g * gw:(g + 1) * gw]
        ms_ = jnp.sum(gg * gg, axis=-1, keepdims=True) / gw
        outs.append(gg * lax.rsqrt(ms_ + EPS))
    o_ref[...] = (jnp.concatenate(outs, axis=1) * ng_ref[...]).astype(o_ref.dtype)


def _ssd(proj, dt_raw, conv_w, conv_b, dt_bias, a_log, d_skip, norm_g, *, z_col, xs_col, b_col, c_col, hp):
    s_len = proj.shape[0]
    nc = s_len // CHUNK
    n_heads = hp // HEAD_DIM
    bc = SSM_GROUPS * SSM_STATE
    row = lambda v: v.reshape(1, -1).astype(F32)
    pad_heads = lambda v: jnp.pad(v.astype(F32), (0, LANES - n_heads)).reshape(1, LANES)
    cw_x, cw_b, cw_c = conv_w[:, :hp], conv_w[:, hp:hp + bc], conv_w[:, hp + bc:]
    cb_x, cb_b, cb_c = row(conv_b[:hp]), row(conv_b[hp:hp + bc]), row(conv_b[hp + bc:])
    a_neg = pad_heads(-jnp.exp(a_log.astype(F32)))
    const = lambda shape: pl.BlockSpec(shape, lambda c: (0, 0))
    return pl.pallas_call(
        _ssd_kernel,
        grid=(nc,),
        in_specs=[pl.BlockSpec((CHUNK, hp), lambda c: (c, z_col // hp)),
                  pl.BlockSpec((CHUNK, hp), lambda c: (c, xs_col // hp)),
                  pl.BlockSpec((CHUNK, bc), lambda c: (c, b_col // bc)),
                  pl.BlockSpec((CHUNK, bc), lambda c: (c, c_col // bc)),
                  pl.BlockSpec((CHUNK, LANES), lambda c: (c, 0)),
                  const((SSM_CONV, hp)), const((SSM_CONV, bc)), const((SSM_CONV, bc)),
                  const((1, hp)), const((1, bc)), const((1, bc)),
                  const((1, LANES)), const((1, LANES)), const((1, hp)), const((1, hp))],
        out_specs=pl.BlockSpec((CHUNK, hp), lambda c: (c, 0)),
        out_shape=jax.ShapeDtypeStruct((s_len, hp), BF16),
        scratch_shapes=[pltpu.VMEM((CHUNK + 8, hp), F32),
                        pltpu.VMEM((CHUNK + 8, bc), F32),
                        pltpu.VMEM((CHUNK + 8, bc), F32),
                        pltpu.VMEM((SSM_STATE, hp), F32)],
        compiler_params=_cparams(1),
        name="ssd_mixer",
    )(proj, proj, proj, proj, dt_raw,
      cw_x.astype(F32), cw_b.astype(F32), cw_c.astype(F32), cb_x, cb_b, cb_c,
      pad_heads(dt_bias), a_neg, row(jnp.repeat(d_skip, HEAD_DIM)), row(norm_g))


def _gate_kernel(u_ref, v_ref, mu_ref, rs_ref, lg_ref, lb_ref, ws_ref, bst_ref, o_ref):
    L = CHUNK
    width = u_ref.shape[1]
    gw = width // GM_GROUPS
    reps = gw // LANES
    mu = _lane_tile(mu_ref[...], reps)
    rs = _lane_tile(rs_ref[...], reps)
    ri = lax.broadcasted_iota(jnp.int32, (L, L), 0)
    ci = lax.broadcasted_iota(jnp.int32, (L, L), 1)
    causal = ci <= ri
    for g in range(GM_GROUPS):
        sl = slice(g * gw, (g + 1) * gw)
        vn = (v_ref[:, sl].astype(F32) - mu) * rs * lg_ref[:, sl] + lb_ref[:, sl]
        wc = jnp.where(causal, ws_ref[g], 0.0).astype(BF16)
        sv = _dot(wc, vn.astype(BF16)) + bst_ref[:, g:g + 1]
        o_ref[:, sl] = (u_ref[:, sl].astype(F32) * sv).astype(o_ref.dtype)


def _gmlp_gate(u, v, mu, rs, ln_g, ln_b, ws, bs):
    s_len, width = u.shape
    nc = s_len // CHUNK
    row = lambda t: t.reshape(1, width).astype(F32)
    return pl.pallas_call(
        _gate_kernel,
        grid=(nc,),
        in_specs=[pl.BlockSpec((CHUNK, width), lambda c: (c, 0)),
                  pl.BlockSpec((CHUNK, width), lambda c: (c, 0)),
                  pl.BlockSpec((CHUNK, LANES), lambda c: (c, 0)),
                  pl.BlockSpec((CHUNK, LANES), lambda c: (c, 0)),
                  pl.BlockSpec((1, width), lambda c: (0, 0)),
                  pl.BlockSpec((1, width), lambda c: (0, 0)),
                  pl.BlockSpec((GM_GROUPS, CHUNK, CHUNK), lambda c: (0, 0, 0)),
                  pl.BlockSpec((CHUNK, GM_GROUPS), lambda c: (0, 0))],
        out_specs=pl.BlockSpec((CHUNK, width), lambda c: (c, 0)),
        out_shape=jax.ShapeDtypeStruct((s_len, width), BF16),
        compiler_params=_cparams(1),
        name="gmlp_gate",
    )(u, v, mu, rs, row(ln_g), row(ln_b), ws.astype(F32), bs.T.astype(F32))


def kernel(x, ffn_norm, ffn_w13, ffn_w2, mix_norm, hy_w_in, attn_q_norm, attn_k_norm, attn_sinks,
           ssm_conv_w, ssm_conv_b, ssm_dt_bias, ssm_A_log, ssm_D, ssm_norm, hy_w_out,
           gm_w_in, gm_ln_g, gm_ln_b, gm_ws, gm_bs, gm_w_out):
    bsz, s_len, d = x.shape
    depth = ffn_norm.shape[0]
    n_attn_heads = attn_sinks.shape[1]
    n_ssm_heads = ssm_D.shape[1]
    q_width = n_attn_heads * HEAD_DIM
    kv_width = (n_attn_heads // KV_GROUP) * HEAD_DIM
    hp = n_ssm_heads * HEAD_DIM
    bc = SSM_GROUPS * SSM_STATE
    assert bsz == 1 and s_len % CHUNK == 0

    bm = min(1024, s_len)
    bn = 512

    xm = x.reshape(s_len, d)
    hb, rstd = _prep(xm, ffn_norm[0, 0])

    for l in range(depth):
        i = l // 2
        g_act = _ffn_up(hb, rstd, ffn_w13[l, 0].astype(BF16), bm=bm, bn=bn)
        xm, hb, rstd = _residual_proj([g_act], ffn_w2[l, 0].astype(BF16), xm, mix_norm[l],
                                      alpha=0.5, bm=bm, bn=bn)
        if l % 2 == 0:
            w = hy_w_in[i]
            o_k = q_width
            o_v = o_k + kv_width
            o_z = o_v + kv_width
            o_x = o_z + hp
            o_b = o_x + hp
            o_c = o_b + bc
            o_dt = o_c + bc
            w_perm = jnp.concatenate([w[:, :o_k], w[:, o_z:o_x], w[:, o_x:o_b], w[:, o_k:o_v],
                                      w[:, o_v:o_z], w[:, o_b:o_c], w[:, o_c:o_dt]], axis=1).astype(BF16)
            w_dt = jnp.pad(w[:, o_dt:], ((0, 0), (0, LANES - n_ssm_heads))).astype(BF16)
            proj, dt_raw = _mixer_in(hb, rstd, w_perm, w_dt, bm=bm, bn=bn)
            p_z, p_x = q_width, q_width + hp
            p_k = p_x + hp
            p_v = p_k + kv_width
            p_b = p_v + kv_width
            p_c = p_b + bc
            attn = _attention(proj, attn_q_norm[i], attn_k_norm[i], attn_sinks[i],
                              q_width=q_width, k_col=p_k, v_col=p_v, kv_width=kv_width)
            ssm = _ssd(proj, dt_raw, ssm_conv_w[i], ssm_conv_b[i], ssm_dt_bias[i], ssm_A_log[i],
                       ssm_D[i], ssm_norm[i], z_col=p_z, xs_col=p_x, b_col=p_b, c_col=p_c, hp=hp)
            xm, hb, rstd = _residual_proj([attn, ssm], hy_w_out[i].astype(BF16), xm, ffn_norm[l, 1],
                                          alpha=1.0, bm=bm, bn=bn)
        else:
            u, v, mu, rs = _gmlp_in(hb, rstd, gm_w_in[i].astype(BF16), bm=bm, bn=bn)
            a = _gmlp_gate(u, v, mu, rs, gm_ln_g[i], gm_ln_b[i], gm_ws[i], gm_bs[i])
            xm, hb, rstd = _residual_proj([a], gm_w_out[i].astype(BF16), xm, ffn_norm[l, 1],
                                          alpha=1.0, bm=bm, bn=bn)
        g_act = _ffn_up(hb, rstd, ffn_w13[l, 1].astype(BF16), bm=bm, bn=bn)
        nxt = ffn_norm[l + 1, 0] if l + 1 < depth else None
        xm, hb, rstd = _residual_proj([g_act], ffn_w2[l, 1].astype(BF16), xm, nxt,
                                      alpha=0.5, bm=bm, bn=bn)
    return xm.reshape(bsz, s_len, d)
```

```python
import functools

import jax
import jax.numpy as jnp
from jax import lax
from jax.experimental import pallas as pl
from jax.experimental.pallas import tpu as pltpu

F32 = jnp.float32
BF16 = jnp.bfloat16

EPS = 1e-6
LANES = 128
HEAD_DIM = 64
WINDOW = 128
KV_GROUP = 8
SSM_GROUPS = 4
SSM_STATE = 128
SSM_CONV = 4
CHUNK = 128
GM_GROUPS = 8
NEG_BIG = -1e30
VMEM_LIMIT = 56 * 1024 * 1024
ROW_TILE = 1024
COL_TILE = 512


def _cparams(n_axes):
    return pltpu.CompilerParams(dimension_semantics=("arbitrary",) * n_axes,
                                vmem_limit_bytes=VMEM_LIMIT)


def _lane_tile(v, reps):
    return v if reps == 1 else jnp.concatenate([v] * reps, axis=1)


def _dot(a, b):
    return jnp.dot(a, b, preferred_element_type=F32)


def _split2(v):
    hi = v.astype(BF16)
    lo = (v - hi.astype(F32)).astype(BF16)
    return hi, lo


def _split3(v):
    hi = v.astype(BF16)
    r = v - hi.astype(F32)
    mid = r.astype(BF16)
    lo = (r - mid.astype(F32)).astype(BF16)
    return hi, mid, lo


def _sigmoid(v):
    return 1.0 / (1.0 + jnp.exp(-v))


def _silu(v):
    return v * _sigmoid(v)


def _gelu_tanh(v):
    c = 0.7978845608028654
    return 0.5 * v * (1.0 + jnp.tanh(c * (v + 0.044715 * (v * v * v))))


def _row_sumsq_lanes(v):
    n = v.shape[1] // LANES
    acc = v[:, :LANES] * v[:, :LANES]
    for t in range(1, n):
        s = v[:, t * LANES:(t + 1) * LANES]
        acc = acc + s * s
    return acc


def _prep_kernel(x_ref, g_ref, hb_ref, rstd_ref, *, d_model):
    x = x_ref[...]
    hb_ref[...] = (x * g_ref[...]).astype(BF16)
    ssq = jnp.sum(x * x, axis=-1, keepdims=True)
    rstd_ref[...] = jnp.broadcast_to(lax.rsqrt(ssq / d_model + EPS), rstd_ref.shape)


def _prep(x, gain):
    m, d = x.shape
    bm = min(256, m)
    return pl.pallas_call(
        functools.partial(_prep_kernel, d_model=d),
        grid=(m // bm,),
        in_specs=[pl.BlockSpec((bm, d), lambda i: (i, 0)),
                  pl.BlockSpec((1, d), lambda i: (0, 0))],
        out_specs=[pl.BlockSpec((bm, d), lambda i: (i, 0)),
                   pl.BlockSpec((bm, LANES), lambda i: (i, 0))],
        out_shape=[jax.ShapeDtypeStruct((m, d), BF16),
                   jax.ShapeDtypeStruct((m, LANES), F32)],
        compiler_params=_cparams(1),
        name="prep_norm",
    )(x, gain.reshape(1, d))


def _rstd_kernel(part_ref, rstd_ref, *, d_model):
    ssq = jnp.sum(part_ref[...], axis=-1, keepdims=True)
    rstd_ref[...] = jnp.broadcast_to(lax.rsqrt(ssq / d_model + EPS), rstd_ref.shape)


def _rstd_from_partials(part, d_model):
    m, pw = part.shape
    bm = min(512, m)
    return pl.pallas_call(
        functools.partial(_rstd_kernel, d_model=d_model),
        grid=(m // bm,),
        in_specs=[pl.BlockSpec((bm, pw), lambda i: (i, 0))],
        out_specs=pl.BlockSpec((bm, LANES), lambda i: (i, 0)),
        out_shape=jax.ShapeDtypeStruct((m, LANES), F32),
        compiler_params=_cparams(1),
        name="rstd_finalize",
    )(part)


def _stream_step(w_chunk_refs, wbuf_ref, compute):
    jj = pl.program_id(0)
    i = pl.program_id(1)
    kc = w_chunk_refs[0].shape[0]
    slot = jj % 2

    def cast():
        row0 = pl.multiple_of(i * kc, kc)
        for w, ref in enumerate(w_chunk_refs):
            wbuf_ref[slot, w, pl.ds(row0, kc), :] = ref[...].astype(BF16)

    @pl.when(jj == 0)
    def _():
        cast()

    @pl.when(jj > 0)
    def _():
        cast()
        compute([wbuf_ref.at[1 - slot, w] for w in range(len(w_chunk_refs))])


def _stream_maps(ni, nj, lead, col_offsets, dest=None):
    def act_idx(jj, i):
        return (jnp.where(jj == 0, 0, i), 0)

    def out_idx(jj, i):
        j = jnp.maximum(jj - 1, 0)
        return (jnp.where(jj == 0, 0, i), j if dest is None else dest(j))

    def w_idx(off):
        return lambda jj, i: lead + (jnp.where(jj == nj, ni - 1, i), jnp.minimum(jj, nj - 1) + off)

    return act_idx, out_idx, [w_idx(off) for off in col_offsets]


def _tiles(m, k):
    bm = min(ROW_TILE, m)
    ni = m // bm
    assert m % bm == 0 and k % ni == 0 and (k // ni) % 16 == 0
    return bm, ni, k // ni


def _pair_kernel(hb_ref, rstd_ref, wa_ref, wb_ref, *rest, act):
    out_refs, wbuf_ref = rest[:-1], rest[-1]

    def compute(w):
        hb = hb_ref[...]
        r = _lane_tile(rstd_ref[...], out_refs[0].shape[1] // LANES)
        a = _dot(hb, w[0][...]) * r
        b = _dot(hb, w[1][...]) * r
        if act == "swiglu":
            out_refs[0][...] = (_silu(a) * b).astype(out_refs[0].dtype)
        else:
            out_refs[0][...] = _gelu_tanh(a).astype(out_refs[0].dtype)
            out_refs[1][...] = _gelu_tanh(b).astype(out_refs[1].dtype)

    _stream_step([wa_ref, wb_ref], wbuf_ref, compute)


def _pair_proj(hb, rstd, w_all, lead, *, act, name):
    m, d = hb.shape
    width = w_all.shape[-1] // 2
    bn = COL_TILE
    bm, ni, kc = _tiles(m, d)
    nj = width // bn
    act_idx, out_idx, (wa_idx, wb_idx) = _stream_maps(ni, nj, lead, [0, nj])
    n_out = 1 if act == "swiglu" else 2
    wspec = lambda idx: pl.BlockSpec((None,) * len(lead) + (kc, bn), idx)
    outs = pl.pallas_call(
        functools.partial(_pair_kernel, act=act),
        grid=(nj + 1, ni),
        in_specs=[pl.BlockSpec((bm, d), act_idx), pl.BlockSpec((bm, LANES), act_idx),
                  wspec(wa_idx), wspec(wb_idx)],
        out_specs=[pl.BlockSpec((bm, bn), out_idx)] * n_out,
        out_shape=[jax.ShapeDtypeStruct((m, width), BF16)] * n_out,
        scratch_shapes=[pltpu.VMEM((2, 2, d, bn), BF16)],
        compiler_params=_cparams(2),
        name=name,
    )(hb, rstd, w_all, w_all)
    return outs[0] if n_out == 1 else outs


def _mixer_in_kernel(hb_ref, rstd_ref, w_ref, proj_ref, wbuf_ref):
    def compute(w):
        acc = _dot(hb_ref[...], w[0][...])
        proj_ref[...] = (acc * _lane_tile(rstd_ref[...], proj_ref.shape[1] // LANES)).astype(proj_ref.dtype)

    _stream_step([w_ref], wbuf_ref, compute)


def _mixer_in(hb, rstd, w_all, lead, *, n_cols, dest):
    m, d = hb.shape
    bn = COL_TILE
    bm, ni, kc = _tiles(m, d)
    nj = n_cols // bn
    act_idx, out_idx, (w_idx,) = _stream_maps(ni, nj, lead, [0], dest)
    return pl.pallas_call(
        _mixer_in_kernel,
        grid=(nj + 1, ni),
        in_specs=[pl.BlockSpec((bm, d), act_idx), pl.BlockSpec((bm, LANES), act_idx),
                  pl.BlockSpec((None,) * len(lead) + (kc, bn), w_idx)],
        out_specs=pl.BlockSpec((bm, bn), out_idx),
        out_shape=jax.ShapeDtypeStruct((m, n_cols), BF16),
        scratch_shapes=[pltpu.VMEM((2, 1, d, bn), BF16)],
        compiler_params=_cparams(2),
        name="mixer_in",
    )(hb, rstd, w_all)


def _residual_kernel(*refs, n_a, alpha, with_norm):
    a_refs = refs[:n_a]
    w_ref, x_ref = refs[n_a], refs[n_a + 1]
    if with_norm:
        g_ref, xo_ref, hb_ref, part_ref, wbuf_ref = refs[n_a + 2:]
    else:
        xo_ref, wbuf_ref = refs[n_a + 2:]

    def compute(w):
        ka = a_refs[0].shape[1]
        acc = _dot(a_refs[0][...], w[0][0:ka, :])
        for k in range(1, n_a):
            acc = acc + _dot(a_refs[k][...], w[0][k * ka:(k + 1) * ka, :])
        xn = x_ref[...] + (acc if alpha == 1.0 else alpha * acc)
        xo_ref[...] = xn
        if with_norm:
            hb_ref[...] = (xn * g_ref[...]).astype(hb_ref.dtype)
            part_ref[...] = _row_sumsq_lanes(xn)

    _stream_step([w_ref], wbuf_ref, compute)


def _residual_proj(a_list, w_all, lead, x, gain_next, *, alpha):
    m, d = x.shape
    n_a = len(a_list)
    ka = a_list[0].shape[1]
    k_total = ka * n_a
    assert all(a.shape[1] == ka for a in a_list) and k_total == w_all.shape[-2]
    bn = COL_TILE
    bm, ni, kc = _tiles(m, k_total)
    nj = d // bn
    with_norm = gain_next is not None
    act_idx, out_idx, (w_idx,) = _stream_maps(ni, nj, lead, [0])
    in_specs = [pl.BlockSpec((bm, ka), act_idx) for _ in range(n_a)]
    in_specs += [pl.BlockSpec((None,) * len(lead) + (kc, bn), w_idx), pl.BlockSpec((bm, bn), out_idx)]
    operands = list(a_list) + [w_all, x]
    out_specs = [pl.BlockSpec((bm, bn), out_idx)]
    out_shape = [jax.ShapeDtypeStruct((m, d), F32)]
    if with_norm:
        in_specs += [pl.BlockSpec((1, bn), lambda jj, i: (0, jnp.maximum(jj - 1, 0)))]
        operands += [gain_next.reshape(1, d)]
        out_specs += [pl.BlockSpec((bm, bn), out_idx), pl.BlockSpec((bm, LANES), out_idx)]
        out_shape += [jax.ShapeDtypeStruct((m, d), BF16),
                      jax.ShapeDtypeStruct((m, nj * LANES), F32)]
    outs = pl.pallas_call(
        functools.partial(_residual_kernel, n_a=n_a, alpha=alpha, with_norm=with_norm),
        grid=(nj + 1, ni),
        in_specs=in_specs,
        out_specs=out_specs,
        out_shape=out_shape,
        scratch_shapes=[pltpu.VMEM((2, 1, k_total, bn), BF16)],
        compiler_params=_cparams(2),
        name="residual_proj",
    )(*operands)
    if with_norm:
        return outs[0], outs[1], _rstd_from_partials(outs[2], d)
    return outs[0], None, None


def _dt_kernel(hb_ref, rstd_ref, w_ref, dt_ref, *, n_heads):
    lane = lax.broadcasted_iota(jnp.int32, w_ref.shape, 1)
    w = jnp.where(lane < n_heads, w_ref[...], 0.0).astype(BF16)
    dt_ref[...] = _dot(hb_ref[...], w) * rstd_ref[...]


def _dt_proj(hb, rstd, w_all, lead, *, dt_col, n_heads):
    m, d = hb.shape
    bm = min(ROW_TILE, m)
    assert dt_col % LANES == 0
    return pl.pallas_call(
        functools.partial(_dt_kernel, n_heads=n_heads),
        grid=(m // bm,),
        in_specs=[pl.BlockSpec((bm, d), lambda i: (i, 0)),
                  pl.BlockSpec((bm, LANES), lambda i: (i, 0)),
                  pl.BlockSpec((None,) * len(lead) + (d, LANES), lambda i: lead + (0, dt_col // LANES))],
        out_specs=pl.BlockSpec((bm, LANES), lambda i: (i, 0)),
        out_shape=jax.ShapeDtypeStruct((m, LANES), F32),
        compiler_params=_cparams(1),
        name="dt_proj",
    )(hb, rstd, w_all)


def _head_ones():
    r = lax.broadcasted_iota(jnp.int32, (LANES, LANES), 0) // HEAD_DIM
    c = lax.broadcasted_iota(jnp.int32, (LANES, LANES), 1) // HEAD_DIM
    return (r == c).astype(BF16)


def _head_rms(v, gain_row):
    rows, width = v.shape
    nt = width // LANES
    ones = _head_ones()
    sq = v * v
    stacked = jnp.concatenate([sq[:, t * LANES:(t + 1) * LANES] for t in range(nt)], axis=0)
    hi, lo = _split2(stacked)
    ssq = _dot(hi, ones) + _dot(lo, ones)
    rs = lax.rsqrt(ssq / HEAD_DIM + EPS)
    return [v[:, t * LANES:(t + 1) * LANES] * rs[t * rows:(t + 1) * rows] * gain_row for t in range(nt)]


def _attn_kernel(q_ref, kp_ref, kc_ref, vp_ref, vc_ref, qg_ref, kg_ref, sink_ref, o_ref):
    n = pl.program_id(0)
    w = WINDOW
    n_kv = kc_ref.shape[1] // HEAD_DIM
    n_tiles = q_ref.shape[1] // LANES
    hd = HEAD_DIM

    qg = qg_ref[...]
    qn = []
    for t in range(n_tiles):
        qt = q_ref[:, t * LANES:(t + 1) * LANES].astype(F32).T
        for half in range(2):
            blk = qt[half * hd:(half + 1) * hd, :]
            rs = lax.rsqrt(jnp.sum(blk * blk, axis=0, keepdims=True) / hd + EPS)
            qn.append((blk * rs * qg).astype(BF16))

    kk = jnp.concatenate([kp_ref[...], kc_ref[...]], axis=0).astype(F32)
    k_tiles = [kt.astype(BF16) for kt in _head_rms(kk, kg_ref[...])]
    vv = jnp.concatenate([vp_ref[...], vc_ref[...]], axis=0).astype(F32)
    vt_tiles = [vv[:, a * LANES:(a + 1) * LANES].T.astype(BF16) for a in range(n_kv // 2)]

    kj = lax.broadcasted_iota(jnp.int32, (2 * w, w), 0)
    qi = lax.broadcasted_iota(jnp.int32, (2 * w, w), 1)
    first_key = jnp.where(n > 0, 0, w)
    valid = (kj > qi) & (kj <= qi + w) & (kj >= first_key)
    zeros = jnp.zeros((hd, w), BF16)

    for h in range(n_kv):
        a, half = h // 2, h % 2
        cols = []
        for g in range(KV_GROUP):
            qh = qn[h * KV_GROUP + g]
            cols.append(jnp.concatenate([qh, zeros] if half == 0 else [zeros, qh], axis=0))
        st = _dot(k_tiles[a], jnp.concatenate(cols, axis=1))
        p_cols, inv_cols = [], []
        for g in range(KV_GROUP):
            s = jnp.where(valid, st[:, g * w:(g + 1) * w], NEG_BIG)
            sk = sink_ref[h:h + 1, g * w:(g + 1) * w]
            mx = jnp.maximum(jnp.max(s, axis=0, keepdims=True), sk)
            p = jnp.exp(s - mx)
            den = jnp.sum(p, axis=0, keepdims=True) + jnp.exp(sk - mx)
            p_cols.append(p.astype(BF16))
            inv_cols.append(1.0 / den)
        vt = vt_tiles[a][half * hd:(half + 1) * hd, :]
        ot = _dot(vt, jnp.concatenate(p_cols, axis=1)) * jnp.concatenate(inv_cols, axis=1)
        for pr in range(KV_GROUP // 2):
            pair = jnp.concatenate([ot[:, (2 * pr) * w:(2 * pr + 1) * w],
                                    ot[:, (2 * pr + 1) * w:(2 * pr + 2) * w]], axis=0)
            t = h * (KV_GROUP // 2) + pr
            o_ref[:, t * LANES:(t + 1) * LANES] = pair.T.astype(o_ref.dtype)


def _attention(proj, q_gain, k_gain, sinks, *, q_width, k_col, v_col, kv_width):
    s_len = proj.shape[0]
    nb = s_len // WINDOW
    n_kv = kv_width // HEAD_DIM
    kb, vb = k_col // kv_width, v_col // kv_width
    qg = jnp.broadcast_to((q_gain.astype(F32) * HEAD_DIM ** -0.5)[:, None], (HEAD_DIM, WINDOW))
    kg = jnp.concatenate([k_gain, k_gain]).reshape(1, LANES).astype(F32)
    sink_rows = jnp.repeat(sinks.astype(F32), WINDOW).reshape(n_kv, KV_GROUP * WINDOW)
    prev = lambda n: jnp.maximum(n - 1, 0)
    const = lambda shape: pl.BlockSpec(shape, lambda n: (0, 0))
    return pl.pallas_call(
        _attn_kernel,
        grid=(nb,),
        in_specs=[pl.BlockSpec((WINDOW, q_width), lambda n: (n, 0)),
                  pl.BlockSpec((WINDOW, kv_width), lambda n: (prev(n), kb)),
                  pl.BlockSpec((WINDOW, kv_width), lambda n: (n, kb)),
                  pl.BlockSpec((WINDOW, kv_width), lambda n: (prev(n), vb)),
                  pl.BlockSpec((WINDOW, kv_width), lambda n: (n, vb)),
                  const((HEAD_DIM, WINDOW)), const((1, LANES)), const((n_kv, KV_GROUP * WINDOW))],
        out_specs=pl.BlockSpec((WINDOW, q_width), lambda n: (n, 0)),
        out_shape=jax.ShapeDtypeStruct((s_len, q_width), BF16),
        compiler_params=_cparams(1),
        name="swa_attention",
    )(proj, proj, proj, proj, proj, qg, kg, sink_rows)


def _dot_nt(a, b):
    return lax.dot_general(a, b, (((1,), (1,)), ((), ())), preferred_element_type=F32)


def _ssd_kernel(z_ref, xs_ref, b_ref, c_ref, dt_ref, cwx_ref, cwb_ref, cwc_ref, cbx_ref, cbb_ref, cbc_ref,
                dtb_ref, a_ref, dsk_ref, ng_ref, o_ref,
                px_ref, pb_ref, pc_ref, state_ref):
    c_idx = pl.program_id(0)
    L = CHUNK
    hp = xs_ref.shape[1]
    n_heads = hp // HEAD_DIM
    gw = hp // SSM_GROUPS
    hist = 8

    @pl.when(c_idx == 0)
    def _():
        px_ref[0:hist, :] = jnp.zeros((hist, px_ref.shape[1]), F32)
        pb_ref[0:hist, :] = jnp.zeros((hist, pb_ref.shape[1]), F32)
        pc_ref[0:hist, :] = jnp.zeros((hist, pc_ref.shape[1]), F32)
        state_ref[...] = jnp.zeros(state_ref.shape, F32)

    def conv_silu(src_ref, pad_ref, w_ref, bias_ref):
        pad_ref[hist:hist + L, :] = src_ref[...].astype(F32)
        acc = bias_ref[...]
        for k in range(SSM_CONV):
            acc = acc + pad_ref[pl.ds(hist - (SSM_CONV - 1) + k, L), :] * w_ref[k:k + 1, :]
        pad_ref[0:hist, :] = pad_ref[L:L + hist, :]
        return _silu(acc)

    xs = conv_silu(xs_ref, px_ref, cwx_ref, cbx_ref)
    bm = conv_silu(b_ref, pb_ref, cwb_ref, cbb_ref)
    cm = conv_silu(c_ref, pc_ref, cwc_ref, cbc_ref)

    dt_in = dt_ref[...] + dtb_ref[...]
    dt = jnp.maximum(dt_in, 0.0) + jnp.log1p(jnp.exp(-jnp.abs(dt_in)))
    da = dt * a_ref[...]
    ri = lax.broadcasted_iota(jnp.int32, (L, L), 0)
    ci = lax.broadcasted_iota(jnp.int32, (L, L), 1)
    causal = ci <= ri
    tril = causal.astype(BF16)
    p0, p1, p2 = _split3(da)
    a_cs = _dot(tril, p0) + _dot(tril, p1) + _dot(tril, p2)
    a_cs_t = a_cs.T
    a_last = a_cs[L - 1:L, :]
    decay_end = jnp.exp(a_last - a_cs)
    e_acs = jnp.exp(a_cs)

    er = lax.broadcasted_iota(jnp.int32, (LANES, hp), 0)
    ec = lax.broadcasted_iota(jnp.int32, (LANES, hp), 1) // HEAD_DIM
    expand = (er == ec).astype(BF16)

    def expand_heads(v):
        hi, lo = _split2(v)
        return _dot(hi, expand) + _dot(lo, expand)

    dt_e = expand_heads(dt)
    xdt = xs * dt_e
    x_end = (xdt * expand_heads(decay_end)).astype(BF16)
    e_acs_e = expand_heads(e_acs)
    chunk_decay_e = e_acs_e[L - 1:L, :]
    xdt_b = xdt.astype(BF16)

    lane2 = lax.broadcasted_iota(jnp.int32, (L, LANES), 1)
    low_half = lane2 < HEAD_DIM
    y_tiles = []
    for g in range(SSM_GROUPS):
        bg = bm[:, g * SSM_STATE:(g + 1) * SSM_STATE]
        cg = cm[:, g * SSM_STATE:(g + 1) * SSM_STATE].astype(BF16)
        cb = _dot_nt(cg, bg.astype(BF16))
        heads_per_group = n_heads // SSM_GROUPS
        for pr in range(heads_per_group // 2):
            h0 = g * heads_per_group + 2 * pr
            t = h0 // 2
            ms = []
            for h in (h0, h0 + 1):
                seg = a_cs[:, h:h + 1] - a_cs_t[h:h + 1, :]
                ms.append((cb * jnp.exp(jnp.where(causal, seg, NEG_BIG))).astype(BF16))
            xt = xdt_b[:, t * LANES:(t + 1) * LANES]
            zero = jnp.zeros_like(xt)
            rhs = jnp.concatenate([jnp.where(low_half, xt, zero), jnp.where(low_half, zero, xt)], axis=0)
            y_tiles.append(_dot(jnp.concatenate(ms, axis=1), rhs))
    y_diag = jnp.concatenate(y_tiles, axis=1)

    y_off_tiles, new_state = [], []
    for g in range(SSM_GROUPS):
        sl = slice(g * gw, (g + 1) * gw)
        st = state_ref[:, sl]
        cg = cm[:, g * SSM_STATE:(g + 1) * SSM_STATE].astype(BF16)
        y_off_tiles.append(_dot(cg, st.astype(BF16)))
        bg_t = bm[:, g * SSM_STATE:(g + 1) * SSM_STATE].T.astype(BF16)
        new_state.append(st * chunk_decay_e[:, sl] + _dot(bg_t, x_end[:, sl]))
    y_off = jnp.concatenate(y_off_tiles, axis=1) * e_acs_e
    for g in range(SSM_GROUPS):
        state_ref[:, g * gw:(g + 1) * gw] = new_state[g]

    y = y_diag + y_off + xs * dsk_ref[...]
    gated = y * _silu(z_ref[...].astype(F32))
    outs = []
    for g in range(SSM_GROUPS):
        gg = gated[:, g * gw:(g + 1) * gw]
        ms_ = jnp.sum(gg * gg, axis=-1, keepdims=True) / gw
        outs.append(gg * lax.rsqrt(ms_ + EPS))
    o_ref[...] = (jnp.concatenate(outs, axis=1) * ng_ref[...]).astype(o_ref.dtype)


def _ssd(proj, dt_raw, conv_w, conv_b, dt_bias, a_log, d_skip, norm_g, *, z_col, xs_col, b_col, c_col, hp):
    s_len = proj.shape[0]
    nc = s_len // CHUNK
    n_heads = hp // HEAD_DIM
    bc = SSM_GROUPS * SSM_STATE
    row = lambda v: v.reshape(1, -1).astype(F32)
    pad_heads = lambda v: jnp.pad(v.astype(F32), (0, LANES - n_heads)).reshape(1, LANES)
    cw_x, cw_b, cw_c = conv_w[:, :hp], conv_w[:, hp:hp + bc], conv_w[:, hp + bc:]
    cb_x, cb_b, cb_c = row(conv_b[:hp]), row(conv_b[hp:hp + bc]), row(conv_b[hp + bc:])
    a_neg = pad_heads(-jnp.exp(a_log.astype(F32)))
    const = lambda shape: pl.BlockSpec(shape, lambda c: (0, 0))
    return pl.pallas_call(
        _ssd_kernel,
        grid=(nc,),
        in_specs=[pl.BlockSpec((CHUNK, hp), lambda c: (c, z_col // hp)),
                  pl.BlockSpec((CHUNK, hp), lambda c: (c, xs_col // hp)),
                  pl.BlockSpec((CHUNK, bc), lambda c: (c, b_col // bc)),
                  pl.BlockSpec((CHUNK, bc), lambda c: (c, c_col // bc)),
                  pl.BlockSpec((CHUNK, LANES), lambda c: (c, 0)),
                  const((SSM_CONV, hp)), const((SSM_CONV, bc)), const((SSM_CONV, bc)),
                  const((1, hp)), const((1, bc)), const((1, bc)),
                  const((1, LANES)), const((1, LANES)), const((1, hp)), const((1, hp))],
        out_specs=pl.BlockSpec((CHUNK, hp), lambda c: (c, 0)),
        out_shape=jax.ShapeDtypeStruct((s_len, hp), BF16),
        scratch_shapes=[pltpu.VMEM((CHUNK + 8, hp), F32),
                        pltpu.VMEM((CHUNK + 8, bc), F32),
                        pltpu.VMEM((CHUNK + 8, bc), F32),
                        pltpu.VMEM((SSM_STATE, hp), F32)],
        compiler_params=_cparams(1),
        name="ssd_mixer",
    )(proj, proj, proj, proj, dt_raw,
      cw_x.astype(F32), cw_b.astype(F32), cw_c.astype(F32), cb_x, cb_b, cb_c,
      pad_heads(dt_bias), a_neg, row(jnp.repeat(d_skip, HEAD_DIM)), row(norm_g))


def _gate_kernel(u_ref, v_ref, lg_ref, lb_ref, ws_ref, bst_ref, o_ref):
    L = CHUNK
    width = u_ref.shape[1]
    gw = width // GM_GROUPS
    groups = [slice(g * gw, (g + 1) * gw) for g in range(GM_GROUPS)]
    part = v_ref[:, groups[0]].astype(F32)
    for sl in groups[1:]:
        part = part + v_ref[:, sl].astype(F32)
    mean = jnp.sum(part, axis=-1, keepdims=True) / width
    sq = None
    for sl in groups:
        dvn = v_ref[:, sl].astype(F32) - mean
        sq = dvn * dvn if sq is None else sq + dvn * dvn
    rs = lax.rsqrt(jnp.sum(sq, axis=-1, keepdims=True) / width + EPS)
    ri = lax.broadcasted_iota(jnp.int32, (L, L), 0)
    ci = lax.broadcasted_iota(jnp.int32, (L, L), 1)
    causal = ci <= ri
    for g, sl in enumerate(groups):
        vn = (v_ref[:, sl].astype(F32) - mean) * rs * lg_ref[:, sl] + lb_ref[:, sl]
        wc = jnp.where(causal, ws_ref[g], 0.0).astype(BF16)
        sv = _dot(wc, vn.astype(BF16)) + bst_ref[:, g:g + 1]
        o_ref[:, sl] = (u_ref[:, sl].astype(F32) * sv).astype(o_ref.dtype)


def _gmlp_gate(u, v, ln_g, ln_b, ws, bs):
    s_len, width = u.shape
    nc = s_len // CHUNK
    row = lambda t: t.reshape(1, width).astype(F32)
    return pl.pallas_call(
        _gate_kernel,
        grid=(nc,),
        in_specs=[pl.BlockSpec((CHUNK, width), lambda c: (c, 0)),
                  pl.BlockSpec((CHUNK, width), lambda c: (c, 0)),
                  pl.BlockSpec((1, width), lambda c: (0, 0)),
                  pl.BlockSpec((1, width), lambda c: (0, 0)),
                  pl.BlockSpec((GM_GROUPS, CHUNK, CHUNK), lambda c: (0, 0, 0)),
                  pl.BlockSpec((CHUNK, GM_GROUPS), lambda c: (0, 0))],
        out_specs=pl.BlockSpec((CHUNK, width), lambda c: (c, 0)),
        out_shape=jax.ShapeDtypeStruct((s_len, width), BF16),
        compiler_params=_cparams(1),
        name="gmlp_gate",
    )(u, v, row(ln_g), row(ln_b), ws.astype(F32), bs.T.astype(F32))


def kernel(x, ffn_norm, ffn_w13, ffn_w2, mix_norm, hy_w_in, attn_q_norm, attn_k_norm, attn_sinks,
           ssm_conv_w, ssm_conv_b, ssm_dt_bias, ssm_A_log, ssm_D, ssm_norm, hy_w_out,
           gm_w_in, gm_ln_g, gm_ln_b, gm_ws, gm_bs, gm_w_out):
    bsz, s_len, d = x.shape
    depth = ffn_norm.shape[0]
    n_attn_heads = attn_sinks.shape[1]
    n_ssm_heads = ssm_D.shape[1]
    q_width = n_attn_heads * HEAD_DIM
    kv_width = (n_attn_heads // KV_GROUP) * HEAD_DIM
    hp = n_ssm_heads * HEAD_DIM
    bc = SSM_GROUPS * SSM_STATE
    assert bsz == 1 and s_len % CHUNK == 0

    bn = COL_TILE
    nq, nkv, nzx = q_width // bn, 2 * kv_width // bn, 2 * hp // bn
    assert q_width % bn == 0 and (2 * kv_width) % bn == 0 and hp % bn == 0 and bc % bn == 0

    def proj_dest(j):
        return j + jnp.where(j < nq, 0, jnp.where(j < nq + nkv, nzx, jnp.where(j < nq + nkv + nzx, -nkv, 0)))

    n_proj = q_width + 2 * kv_width + 2 * hp + 2 * bc
    p_z, p_x = q_width, q_width + hp
    p_k = p_x + hp
    p_v = p_k + kv_width
    p_b = p_v + kv_width
    p_c = p_b + bc

    xm = x.reshape(s_len, d)
    hb, rstd = _prep(xm, ffn_norm[0, 0])

    for l in range(depth):
        i = l // 2
        g_act = _pair_proj(hb, rstd, ffn_w13, (l, 0), act="swiglu", name="ffn_up")
        xm, hb, rstd = _residual_proj([g_act], ffn_w2, (l, 0), xm, mix_norm[l], alpha=0.5)
        if l % 2 == 0:
            proj = _mixer_in(hb, rstd, hy_w_in, (i,), n_cols=n_proj, dest=proj_dest)
            dt_raw = _dt_proj(hb, rstd, hy_w_in, (i,), dt_col=n_proj, n_heads=n_ssm_heads)
            attn = _attention(proj, attn_q_norm[i], attn_k_norm[i], attn_sinks[i],
                              q_width=q_width, k_col=p_k, v_col=p_v, kv_width=kv_width)
            ssm = _ssd(proj, dt_raw, ssm_conv_w[i], ssm_conv_b[i], ssm_dt_bias[i], ssm_A_log[i],
                       ssm_D[i], ssm_norm[i], z_col=p_z, xs_col=p_x, b_col=p_b, c_col=p_c, hp=hp)
            xm, hb, rstd = _residual_proj([attn, ssm], hy_w_out, (i,), xm, ffn_norm[l, 1], alpha=1.0)
        else:
            u, v = _pair_proj(hb, rstd, gm_w_in, (i,), act="gelu", name="gmlp_in")
            a = _gmlp_gate(u, v, gm_ln_g[i], gm_ln_b[i], gm_ws[i], gm_bs[i])
            xm, hb, rstd = _residual_proj([a], gm_w_out, (i,), xm, ffn_norm[l, 1], alpha=1.0)
        g_act = _pair_proj(hb, rstd, ffn_w13, (l, 1), act="swiglu", name="ffn_up")
        nxt = ffn_norm[l + 1, 0] if l + 1 < depth else None
        xm, hb, rstd = _residual_proj([g_act], ffn_w2, (l, 1), xm, nxt, alpha=0.5)
    return xm.reshape(bsz, s_len, d)
```

```python
import functools

import jax
import jax.numpy as jnp
from jax import lax
from jax.experimental import pallas as pl
from jax.experimental.pallas import tpu as pltpu

F32 = jnp.float32
BF16 = jnp.bfloat16

EPS = 1e-6
LANES = 128
HEAD_DIM = 64
WINDOW = 128
KV_GROUP = 8
SSM_GROUPS = 4
SSM_STATE = 128
SSM_CONV = 4
CHUNK = 128
GM_GROUPS = 8
NEG_BIG = -1e30
VMEM_LIMIT = 62 * 1024 * 1024
ROW_TILE = 1024
COL_TILE = 512
RES_ROW_TILE = 512
RES_COL_TILE = 1024
SUB_COLS = 256


def _cparams(n_axes):
    return pltpu.CompilerParams(dimension_semantics=("arbitrary",) * n_axes,
                                vmem_limit_bytes=VMEM_LIMIT)


def _lane_tile(v, reps):
    return v if reps == 1 else jnp.concatenate([v] * reps, axis=1)


def _dot(a, b):
    return jnp.dot(a, b, preferred_element_type=F32)


def _split2(v):
    hi = v.astype(BF16)
    lo = (v - hi.astype(F32)).astype(BF16)
    return hi, lo


def _split3(v):
    hi = v.astype(BF16)
    r = v - hi.astype(F32)
    mid = r.astype(BF16)
    lo = (r - mid.astype(F32)).astype(BF16)
    return hi, mid, lo


def _sigmoid(v):
    return 1.0 / (1.0 + jnp.exp(-v))


def _silu(v):
    return v * _sigmoid(v)


def _gelu_tanh(v):
    c = 0.7978845608028654
    return 0.5 * v * (1.0 + jnp.tanh(c * (v + 0.044715 * (v * v * v))))


def _row_sumsq_lanes(v):
    n = v.shape[1] // LANES
    acc = v[:, :LANES] * v[:, :LANES]
    for t in range(1, n):
        s = v[:, t * LANES:(t + 1) * LANES]
        acc = acc + s * s
    return acc


def _prep_kernel(x_ref, g_ref, hb_ref, rstd_ref, *, d_model):
    x = x_ref[...]
    hb_ref[...] = (x * g_ref[...]).astype(BF16)
    ssq = jnp.sum(x * x, axis=-1, keepdims=True)
    rstd_ref[...] = jnp.broadcast_to(lax.rsqrt(ssq / d_model + EPS), rstd_ref.shape)


def _prep(x, gain):
    m, d = x.shape
    bm = min(256, m)
    return pl.pallas_call(
        functools.partial(_prep_kernel, d_model=d),
        grid=(m // bm,),
        in_specs=[pl.BlockSpec((bm, d), lambda i: (i, 0)),
                  pl.BlockSpec((1, d), lambda i: (0, 0))],
        out_specs=[pl.BlockSpec((bm, d), lambda i: (i, 0)),
                   pl.BlockSpec((bm, LANES), lambda i: (i, 0))],
        out_shape=[jax.ShapeDtypeStruct((m, d), BF16),
                   jax.ShapeDtypeStruct((m, LANES), F32)],
        compiler_params=_cparams(1),
        name="prep_norm",
    )(x, gain.reshape(1, d))


def _stream_step(w_chunk_refs, wbuf_ref, compute):
    jj = pl.program_id(0)
    i = pl.program_id(1)
    kc = w_chunk_refs[0].shape[0]

    def cast(slot):
        row0 = pl.multiple_of(i * kc, kc)
        for w, ref in enumerate(w_chunk_refs):
            wbuf_ref[slot, w, pl.ds(row0, kc), :] = ref[...].astype(BF16)

    @pl.when(jj == 0)
    def _():
        cast(0)

    for parity in (0, 1):
        @pl.when((jj > 0) & (jj % 2 == parity))
        def _():
            compute([wbuf_ref.at[1 - parity, w] for w in range(len(w_chunk_refs))])
            cast(parity)


def _stream_maps(ni, nj, lead, col_offsets, dest=None):
    def act_idx(jj, i):
        return (jnp.where(jj == 0, 0, i), 0)

    def out_idx(jj, i):
        j = jnp.maximum(jj - 1, 0)
        return (jnp.where(jj == 0, 0, i), j if dest is None else dest(j))

    def w_idx(off):
        return lambda jj, i: lead + (jnp.where(jj == nj, ni - 1, i), jnp.minimum(jj, nj - 1) + off)

    return act_idx, out_idx, [w_idx(off) for off in col_offsets]


def _tiles(m, k, row_tile=None):
    bm = min(row_tile or ROW_TILE, m)
    ni = m // bm
    assert m % bm == 0 and k % ni == 0 and (k // ni) % 16 == 0
    return bm, ni, k // ni


def _pair_kernel(hb_ref, rstd_ref, wa_ref, wb_ref, *rest, act):
    out_refs, wbuf_ref = rest[:-1], rest[-1]

    def compute(w):
        hb = hb_ref[...]
        bn = out_refs[0].shape[1]
        sub = min(SUB_COLS, bn)
        r = _lane_tile(rstd_ref[...], sub // LANES)
        for c in range(0, bn, sub):
            a = _dot(hb, w[0][:, c:c + sub]) * r
            b = _dot(hb, w[1][:, c:c + sub]) * r
            if act == "swiglu":
                out_refs[0][:, c:c + sub] = (_silu(a) * b).astype(out_refs[0].dtype)
            else:
                out_refs[0][:, c:c + sub] = _gelu_tanh(a).astype(out_refs[0].dtype)
                out_refs[1][:, c:c + sub] = _gelu_tanh(b).astype(out_refs[1].dtype)

    _stream_step([wa_ref, wb_ref], wbuf_ref, compute)


def _pair_proj(hb, rstd, w_all, lead, *, act, name):
    m, d = hb.shape
    width = w_all.shape[-1] // 2
    bn = COL_TILE
    bm, ni, kc = _tiles(m, d)
    nj = width // bn
    act_idx, out_idx, (wa_idx, wb_idx) = _stream_maps(ni, nj, lead, [0, nj])
    n_out = 1 if act == "swiglu" else 2
    wspec = lambda idx: pl.BlockSpec((None,) * len(lead) + (kc, bn), idx)
    outs = pl.pallas_call(
        functools.partial(_pair_kernel, act=act),
        grid=(nj + 1, ni),
        in_specs=[pl.BlockSpec((bm, d), act_idx), pl.BlockSpec((bm, LANES), act_idx),
                  wspec(wa_idx), wspec(wb_idx)],
        out_specs=[pl.BlockSpec((bm, bn), out_idx)] * n_out,
        out_shape=[jax.ShapeDtypeStruct((m, width), BF16)] * n_out,
        scratch_shapes=[pltpu.VMEM((2, 2, d, bn), BF16)],
        compiler_params=_cparams(2),
        name=name,
    )(hb, rstd, w_all, w_all)
    return outs[0] if n_out == 1 else outs


def _mixer_in_kernel(hb_ref, rstd_ref, w_ref, proj_ref, wbuf_ref):
    def compute(w):
        hb = hb_ref[...]
        bn = proj_ref.shape[1]
        sub = min(SUB_COLS, bn)
        r = _lane_tile(rstd_ref[...], sub // LANES)
        for c in range(0, bn, sub):
            proj_ref[:, c:c + sub] = (_dot(hb, w[0][:, c:c + sub]) * r).astype(proj_ref.dtype)

    _stream_step([w_ref], wbuf_ref, compute)


def _mixer_in(hb, rstd, w_all, lead, *, n_cols, dest):
    m, d = hb.shape
    bn = COL_TILE
    bm, ni, kc = _tiles(m, d)
    nj = n_cols // bn
    act_idx, out_idx, (w_idx,) = _stream_maps(ni, nj, lead, [0], dest)
    return pl.pallas_call(
        _mixer_in_kernel,
        grid=(nj + 1, ni),
        in_specs=[pl.BlockSpec((bm, d), act_idx), pl.BlockSpec((bm, LANES), act_idx),
                  pl.BlockSpec((None,) * len(lead) + (kc, bn), w_idx)],
        out_specs=pl.BlockSpec((bm, bn), out_idx),
        out_shape=jax.ShapeDtypeStruct((m, n_cols), BF16),
        scratch_shapes=[pltpu.VMEM((2, 1, d, bn), BF16)],
        compiler_params=_cparams(2),
        name="mixer_in",
    )(hb, rstd, w_all)


def _residual_kernel(*refs, n_a, alpha, with_norm, d_model):
    a_refs = refs[:n_a]
    w_ref, x_ref = refs[n_a], refs[n_a + 1]
    if with_norm:
        g_ref, xo_ref, hb_ref, rstd_ref, wbuf_ref, ssq_ref = refs[n_a + 2:]
    else:
        xo_ref, wbuf_ref = refs[n_a + 2:]
    jj = pl.program_id(0)
    i = pl.program_id(1)

    def compute(w):
        ka = a_refs[0].shape[1]
        bm, bn = xo_ref.shape
        sub = min(SUB_COLS, bn)
        a_vals = [a_ref[...] for a_ref in a_refs]
        part = None
        for c in range(0, bn, sub):
            acc = _dot(a_vals[0], w[0][0:ka, c:c + sub])
            for k in range(1, n_a):
                acc = acc + _dot(a_vals[k], w[0][k * ka:(k + 1) * ka, c:c + sub])
            xn = x_ref[:, c:c + sub] + (acc if alpha == 1.0 else alpha * acc)
            xo_ref[:, c:c + sub] = xn
            if with_norm:
                hb_ref[:, c:c + sub] = (xn * g_ref[:, c:c + sub]).astype(hb_ref.dtype)
                sq = _row_sumsq_lanes(xn)
                part = sq if part is None else part + sq
        if not with_norm:
            return
        rows = pl.ds(pl.multiple_of(i * bm, bm), bm)

        @pl.when(jj == 1)
        def _():
            ssq_ref[rows, :] = part

        @pl.when(jj > 1)
        def _():
            ssq_ref[rows, :] += part

        @pl.when(jj == pl.num_programs(0) - 1)
        def _():
            ssq = jnp.sum(ssq_ref[rows, :], axis=-1, keepdims=True)
            rstd_ref[...] = jnp.broadcast_to(lax.rsqrt(ssq / d_model + EPS), rstd_ref.shape)

    _stream_step([w_ref], wbuf_ref, compute)


def _residual_proj(a_list, w_all, lead, x, gain_next, *, alpha):
    m, d = x.shape
    n_a = len(a_list)
    ka = a_list[0].shape[1]
    k_total = ka * n_a
    assert all(a.shape[1] == ka for a in a_list) and k_total == w_all.shape[-2]
    bn = min(RES_COL_TILE, d)
    bm, ni, kc = _tiles(m, k_total, RES_ROW_TILE)
    nj = d // bn
    with_norm = gain_next is not None
    act_idx, out_idx, (w_idx,) = _stream_maps(ni, nj, lead, [0])
    in_specs = [pl.BlockSpec((bm, ka), act_idx) for _ in range(n_a)]
    in_specs += [pl.BlockSpec((None,) * len(lead) + (kc, bn), w_idx), pl.BlockSpec((bm, bn), out_idx)]
    operands = list(a_list) + [w_all, x]
    out_specs = [pl.BlockSpec((bm, bn), out_idx)]
    out_shape = [jax.ShapeDtypeStruct((m, d), F32)]
    scratch = [pltpu.VMEM((2, 1, k_total, bn), BF16)]
    if with_norm:
        in_specs += [pl.BlockSpec((1, bn), lambda jj, i: (0, jnp.maximum(jj - 1, 0)))]
        operands += [gain_next.reshape(1, d)]
        out_specs += [pl.BlockSpec((bm, bn), out_idx),
                      pl.BlockSpec((bm, LANES), lambda jj, i: (jnp.where(jj == nj, i, 0), 0))]
        out_shape += [jax.ShapeDtypeStruct((m, d), BF16),
                      jax.ShapeDtypeStruct((m, LANES), F32)]
        scratch += [pltpu.VMEM((m, LANES), F32)]
    outs = pl.pallas_call(
        functools.partial(_residual_kernel, n_a=n_a, alpha=alpha, with_norm=with_norm, d_model=d),
        grid=(nj + 1, ni),
        in_specs=in_specs,
        out_specs=out_specs,
        out_shape=out_shape,
        scratch_shapes=scratch,
        compiler_params=_cparams(2),
        name="residual_proj",
    )(*operands)
    if with_norm:
        return outs[0], outs[1], outs[2]
    return outs[0], None, None


def _dt_kernel(hb_ref, rstd_ref, w_ref, dt_ref, *, n_heads):
    lane = lax.broadcasted_iota(jnp.int32, w_ref.shape, 1)
    w = jnp.where(lane < n_heads, w_ref[...], 0.0).astype(BF16)
    dt_ref[...] = _dot(hb_ref[...], w) * rstd_ref[...]


def _dt_proj(hb, rstd, w_all, lead, *, dt_col, n_heads):
    m, d = hb.shape
    bm = min(ROW_TILE, m)
    assert dt_col % LANES == 0
    return pl.pallas_call(
        functools.partial(_dt_kernel, n_heads=n_heads),
        grid=(m // bm,),
        in_specs=[pl.BlockSpec((bm, d), lambda i: (i, 0)),
                  pl.BlockSpec((bm, LANES), lambda i: (i, 0)),
                  pl.BlockSpec((None,) * len(lead) + (d, LANES), lambda i: lead + (0, dt_col // LANES))],
        out_specs=pl.BlockSpec((bm, LANES), lambda i: (i, 0)),
        out_shape=jax.ShapeDtypeStruct((m, LANES), F32),
        compiler_params=_cparams(1),
        name="dt_proj",
    )(hb, rstd, w_all)


def _head_ones():
    r = lax.broadcasted_iota(jnp.int32, (LANES, LANES), 0) // HEAD_DIM
    c = lax.broadcasted_iota(jnp.int32, (LANES, LANES), 1) // HEAD_DIM
    return (r == c).astype(BF16)


def _head_rms(v, gain_row):
    rows, width = v.shape
    nt = width // LANES
    ones = _head_ones()
    sq = v * v
    stacked = jnp.concatenate([sq[:, t * LANES:(t + 1) * LANES] for t in range(nt)], axis=0)
    hi, lo = _split2(stacked)
    ssq = _dot(hi, ones) + _dot(lo, ones)
    rs = lax.rsqrt(ssq / HEAD_DIM + EPS)
    return [v[:, t * LANES:(t + 1) * LANES] * rs[t * rows:(t + 1) * rows] * gain_row for t in range(nt)]


def _attn_kernel(q_ref, kp_ref, kc_ref, vp_ref, vc_ref, qg_ref, kg_ref, sink_ref, o_ref):
    n = pl.program_id(0)
    w = WINDOW
    n_kv = kc_ref.shape[1] // HEAD_DIM
    n_tiles = q_ref.shape[1] // LANES
    hd = HEAD_DIM

    qg = qg_ref[...]
    qn = []
    for t in range(n_tiles):
        qt = q_ref[:, t * LANES:(t + 1) * LANES].astype(F32).T
        for half in range(2):
            blk = qt[half * hd:(half + 1) * hd, :]
            rs = lax.rsqrt(jnp.sum(blk * blk, axis=0, keepdims=True) / hd + EPS)
            qn.append((blk * rs * qg).astype(BF16))

    kk = jnp.concatenate([kp_ref[...], kc_ref[...]], axis=0).astype(F32)
    k_tiles = [kt.astype(BF16) for kt in _head_rms(kk, kg_ref[...])]
    vv = jnp.concatenate([vp_ref[...], vc_ref[...]], axis=0).astype(F32)
    vt_tiles = [vv[:, a * LANES:(a + 1) * LANES].T.astype(BF16) for a in range(n_kv // 2)]

    kj = lax.broadcasted_iota(jnp.int32, (2 * w, w), 0)
    qi = lax.broadcasted_iota(jnp.int32, (2 * w, w), 1)
    first_key = jnp.where(n > 0, 0, w)
    valid = (kj > qi) & (kj <= qi + w) & (kj >= first_key)
    zeros = jnp.zeros((hd, w), BF16)

    for h in range(n_kv):
        a, half = h // 2, h % 2
        cols = []
        for g in range(KV_GROUP):
            qh = qn[h * KV_GROUP + g]
            cols.append(jnp.concatenate([qh, zeros] if half == 0 else [zeros, qh], axis=0))
        st = _dot(k_tiles[a], jnp.concatenate(cols, axis=1))
        p_cols, inv_cols = [], []
        for g in range(KV_GROUP):
            s = jnp.where(valid, st[:, g * w:(g + 1) * w], NEG_BIG)
            sk = sink_ref[h:h + 1, g * w:(g + 1) * w]
            mx = jnp.maximum(jnp.max(s, axis=0, keepdims=True), sk)
            p = jnp.exp(s - mx)
            den = jnp.sum(p, axis=0, keepdims=True) + jnp.exp(sk - mx)
            p_cols.append(p.astype(BF16))
            inv_cols.append(1.0 / den)
        vt = vt_tiles[a][half * hd:(half + 1) * hd, :]
        ot = _dot(vt, jnp.concatenate(p_cols, axis=1)) * jnp.concatenate(inv_cols, axis=1)
        for pr in range(KV_GROUP // 2):
            pair = jnp.concatenate([ot[:, (2 * pr) * w:(2 * pr + 1) * w],
                                    ot[:, (2 * pr + 1) * w:(2 * pr + 2) * w]], axis=0)
            t = h * (KV_GROUP // 2) + pr
            o_ref[:, t * LANES:(t + 1) * LANES] = pair.T.astype(o_ref.dtype)


def _attention(proj, q_gain, k_gain, sinks, *, q_width, k_col, v_col, kv_width):
    s_len = proj.shape[0]
    nb = s_len // WINDOW
    n_kv = kv_width // HEAD_DIM
    kb, vb = k_col // kv_width, v_col // kv_width
    qg = jnp.broadcast_to((q_gain.astype(F32) * HEAD_DIM ** -0.5)[:, None], (HEAD_DIM, WINDOW))
    kg = jnp.concatenate([k_gain, k_gain]).reshape(1, LANES).astype(F32)
    sink_rows = jnp.repeat(sinks.astype(F32), WINDOW).reshape(n_kv, KV_GROUP * WINDOW)
    prev = lambda n: jnp.maximum(n - 1, 0)
    const = lambda shape: pl.BlockSpec(shape, lambda n: (0, 0))
    return pl.pallas_call(
        _attn_kernel,
        grid=(nb,),
        in_specs=[pl.BlockSpec((WINDOW, q_width), lambda n: (n, 0)),
                  pl.BlockSpec((WINDOW, kv_width), lambda n: (prev(n), kb)),
                  pl.BlockSpec((WINDOW, kv_width), lambda n: (n, kb)),
                  pl.BlockSpec((WINDOW, kv_width), lambda n: (prev(n), vb)),
                  pl.BlockSpec((WINDOW, kv_width), lambda n: (n, vb)),
                  const((HEAD_DIM, WINDOW)), const((1, LANES)), const((n_kv, KV_GROUP * WINDOW))],
        out_specs=pl.BlockSpec((WINDOW, q_width), lambda n: (n, 0)),
        out_shape=jax.ShapeDtypeStruct((s_len, q_width), BF16),
        compiler_params=_cparams(1),
        name="swa_attention",
    )(proj, proj, proj, proj, proj, qg, kg, sink_rows)


def _dot_nt(a, b):
    return lax.dot_general(a, b, (((1,), (1,)), ((), ())), preferred_element_type=F32)


def _ssd_kernel(z_ref, xs_ref, b_ref, c_ref, dt_ref, cwx_ref, cwb_ref, cwc_ref, cbx_ref, cbb_ref, cbc_ref,
                dtb_ref, a_ref, dsk_ref, ng_ref, o_ref,
                px_ref, pb_ref, pc_ref, state_ref):
    c_idx = pl.program_id(0)
    L = CHUNK
    hp = xs_ref.shape[1]
    n_heads = hp // HEAD_DIM
    gw = hp // SSM_GROUPS
    hist = 8

    @pl.when(c_idx == 0)
    def _():
        px_ref[0:hist, :] = jnp.zeros((hist, px_ref.shape[1]), F32)
        pb_ref[0:hist, :] = jnp.zeros((hist, pb_ref.shape[1]), F32)
        pc_ref[0:hist, :] = jnp.zeros((hist, pc_ref.shape[1]), F32)
        state_ref[...] = jnp.zeros(state_ref.shape, F32)

    def conv_silu(src_ref, pad_ref, w_ref, bias_ref):
        pad_ref[hist:hist + L, :] = src_ref[...].astype(F32)
        acc = bias_ref[...]
        for k in range(SSM_CONV):
            acc = acc + pad_ref[pl.ds(hist - (SSM_CONV - 1) + k, L), :] * w_ref[k:k + 1, :]
        pad_ref[0:hist, :] = pad_ref[L:L + hist, :]
        return _silu(acc)

    xs = conv_silu(xs_ref, px_ref, cwx_ref, cbx_ref)
    bm = conv_silu(b_ref, pb_ref, cwb_ref, cbb_ref)
    cm = conv_silu(c_ref, pc_ref, cwc_ref, cbc_ref)

    dt_in = dt_ref[...] + dtb_ref[...]
    dt = jnp.maximum(dt_in, 0.0) + jnp.log1p(jnp.exp(-jnp.abs(dt_in)))
    da = dt * a_ref[...]
    ri = lax.broadcasted_iota(jnp.int32, (L, L), 0)
    ci = lax.broadcasted_iota(jnp.int32, (L, L), 1)
    causal = ci <= ri
    tril = causal.astype(BF16)
    p0, p1, p2 = _split3(da)
    a_cs = _dot(tril, p0) + _dot(tril, p1) + _dot(tril, p2)
    a_cs_t = a_cs.T
    a_last = a_cs[L - 1:L, :]
    decay_end = jnp.exp(a_last - a_cs)
    e_acs = jnp.exp(a_cs)

    er = lax.broadcasted_iota(jnp.int32, (LANES, hp), 0)
    ec = lax.broadcasted_iota(jnp.int32, (LANES, hp), 1) // HEAD_DIM
    expand = (er == ec).astype(BF16)

    def expand_heads(v):
        hi, lo = _split2(v)
        return _dot(hi, expand) + _dot(lo, expand)

    dt_e = expand_heads(dt)
    xdt = xs * dt_e
    x_end = (xdt * expand_heads(decay_end)).astype(BF16)
    e_acs_e = expand_heads(e_acs)
    chunk_decay_e = e_acs_e[L - 1:L, :]
    xdt_b = xdt.astype(BF16)

    lane2 = lax.broadcasted_iota(jnp.int32, (L, LANES), 1)
    low_half = lane2 < HEAD_DIM
    y_tiles = []
    for g in range(SSM_GROUPS):
        bg = bm[:, g * SSM_STATE:(g + 1) * SSM_STATE]
        cg = cm[:, g * SSM_STATE:(g + 1) * SSM_STATE].astype(BF16)
        cb = _dot_nt(cg, bg.astype(BF16))
        heads_per_group = n_heads // SSM_GROUPS
        for pr in range(heads_per_group // 2):
            h0 = g * heads_per_group + 2 * pr
            t = h0 // 2
            ms = []
            for h in (h0, h0 + 1):
                seg = a_cs[:, h:h + 1] - a_cs_t[h:h + 1, :]
                ms.append((cb * jnp.exp(jnp.where(causal, seg, NEG_BIG))).astype(BF16))
            xt = xdt_b[:, t * LANES:(t + 1) * LANES]
            zero = jnp.zeros_like(xt)
            rhs = jnp.concatenate([jnp.where(low_half, xt, zero), jnp.where(low_half, zero, xt)], axis=0)
            y_tiles.append(_dot(jnp.concatenate(ms, axis=1), rhs))
    y_diag = jnp.concatenate(y_tiles, axis=1)

    y_off_tiles, new_state = [], []
    for g in range(SSM_GROUPS):
        sl = slice(g * gw, (g + 1) * gw)
        st = state_ref[:, sl]
        cg = cm[:, g * SSM_STATE:(g + 1) * SSM_STATE].astype(BF16)
        y_off_tiles.append(_dot(cg, st.astype(BF16)))
        bg_t = bm[:, g * SSM_STATE:(g + 1) * SSM_STATE].T.astype(BF16)
        new_state.append(st * chunk_decay_e[:, sl] + _dot(bg_t, x_end[:, sl]))
    y_off = jnp.concatenate(y_off_tiles, axis=1) * e_acs_e
    for g in range(SSM_GROUPS):
        state_ref[:, g * gw:(g + 1) * gw] = new_state[g]

    y = y_diag + y_off + xs * dsk_ref[...]
    gated = y * _silu(z_ref[...].astype(F32))
    outs = []
    for g in range(SSM_GROUPS):
        gg = gated[:, g * gw:(g + 1) * gw]
        ms_ = jnp.sum(gg * gg, axis=-1, keepdims=True) / gw
        outs.append(gg * lax.rsqrt(ms_ + EPS))
    o_ref[...] = (jnp.concatenate(outs, axis=1) * ng_ref[...]).astype(o_ref.dtype)


def _ssd(proj, dt_raw, conv_w, conv_b, dt_bias, a_log, d_skip, norm_g, *, z_col, xs_col, b_col, c_col, hp):
    s_len = proj.shape[0]
    nc = s_len // CHUNK
    n_heads = hp // HEAD_DIM
    bc = SSM_GROUPS * SSM_STATE
    row = lambda v: v.reshape(1, -1).astype(F32)
    pad_heads = lambda v: jnp.pad(v.astype(F32), (0, LANES - n_heads)).reshape(1, LANES)
    cw_x, cw_b, cw_c = conv_w[:, :hp], conv_w[:, hp:hp + bc], conv_w[:, hp + bc:]
    cb_x, cb_b, cb_c = row(conv_b[:hp]), row(conv_b[hp:hp + bc]), row(conv_b[hp + bc:])
    a_neg = pad_heads(-jnp.exp(a_log.astype(F32)))
    const = lambda shape: pl.BlockSpec(shape, lambda c: (0, 0))
    return pl.pallas_call(
        _ssd_kernel,
        grid=(nc,),
        in_specs=[pl.BlockSpec((CHUNK, hp), lambda c: (c, z_col // hp)),
                  pl.BlockSpec((CHUNK, hp), lambda c: (c, xs_col // hp)),
                  pl.BlockSpec((CHUNK, bc), lambda c: (c, b_col // bc)),
                  pl.BlockSpec((CHUNK, bc), lambda c: (c, c_col // bc)),
                  pl.BlockSpec((CHUNK, LANES), lambda c: (c, 0)),
                  const((SSM_CONV, hp)), const((SSM_CONV, bc)), const((SSM_CONV, bc)),
                  const((1, hp)), const((1, bc)), const((1, bc)),
                  const((1, LANES)), const((1, LANES)), const((1, hp)), const((1, hp))],
        out_specs=pl.BlockSpec((CHUNK, hp), lambda c: (c, 0)),
        out_shape=jax.ShapeDtypeStruct((s_len, hp), BF16),
        scratch_shapes=[pltpu.VMEM((CHUNK + 8, hp), F32),
                        pltpu.VMEM((CHUNK + 8, bc), F32),
                        pltpu.VMEM((CHUNK + 8, bc), F32),
                        pltpu.VMEM((SSM_STATE, hp), F32)],
        compiler_params=_cparams(1),
        name="ssd_mixer",
    )(proj, proj, proj, proj, dt_raw,
      cw_x.astype(F32), cw_b.astype(F32), cw_c.astype(F32), cb_x, cb_b, cb_c,
      pad_heads(dt_bias), a_neg, row(jnp.repeat(d_skip, HEAD_DIM)), row(norm_g))


def _gate_kernel(u_ref, v_ref, lg_ref, lb_ref, ws_ref, bst_ref, o_ref):
    L = CHUNK
    width = u_ref.shape[1]
    gw = width // GM_GROUPS
    groups = [slice(g * gw, (g + 1) * gw) for g in range(GM_GROUPS)]
    part = v_ref[:, groups[0]].astype(F32)
    for sl in groups[1:]:
        part = part + v_ref[:, sl].astype(F32)
    mean = jnp.sum(part, axis=-1, keepdims=True) / width
    sq = None
    for sl in groups:
        dvn = v_ref[:, sl].astype(F32) - mean
        sq = dvn * dvn if sq is None else sq + dvn * dvn
    rs = lax.rsqrt(jnp.sum(sq, axis=-1, keepdims=True) / width + EPS)
    ri = lax.broadcasted_iota(jnp.int32, (L, L), 0)
    ci = lax.broadcasted_iota(jnp.int32, (L, L), 1)
    causal = ci <= ri
    for g, sl in enumerate(groups):
        vn = (v_ref[:, sl].astype(F32) - mean) * rs * lg_ref[:, sl] + lb_ref[:, sl]
        wc = jnp.where(causal, ws_ref[g], 0.0).astype(BF16)
        sv = _dot(wc, vn.astype(BF16)) + bst_ref[:, g:g + 1]
        o_ref[:, sl] = (u_ref[:, sl].astype(F32) * sv).astype(o_ref.dtype)


def _gmlp_gate(u, v, ln_g, ln_b, ws, bs):
    s_len, width = u.shape
    nc = s_len // CHUNK
    row = lambda t: t.reshape(1, width).astype(F32)
    return pl.pallas_call(
        _gate_kernel,
        grid=(nc,),
        in_specs=[pl.BlockSpec((CHUNK, width), lambda c: (c, 0)),
                  pl.BlockSpec((CHUNK, width), lambda c: (c, 0)),
                  pl.BlockSpec((1, width), lambda c: (0, 0)),
                  pl.BlockSpec((1, width), lambda c: (0, 0)),
                  pl.BlockSpec((GM_GROUPS, CHUNK, CHUNK), lambda c: (0, 0, 0)),
                  pl.BlockSpec((CHUNK, GM_GROUPS), lambda c: (0, 0))],
        out_specs=pl.BlockSpec((CHUNK, width), lambda c: (c, 0)),
        out_shape=jax.ShapeDtypeStruct((s_len, width), BF16),
        compiler_params=_cparams(1),
        name="gmlp_gate",
    )(u, v, row(ln_g), row(ln_b), ws.astype(F32), bs.T.astype(F32))


def kernel(x, ffn_norm, ffn_w13, ffn_w2, mix_norm, hy_w_in, attn_q_norm, attn_k_norm, attn_sinks,
           ssm_conv_w, ssm_conv_b, ssm_dt_bias, ssm_A_log, ssm_D, ssm_norm, hy_w_out,
           gm_w_in, gm_ln_g, gm_ln_b, gm_ws, gm_bs, gm_w_out):
    bsz, s_len, d = x.shape
    depth = ffn_norm.shape[0]
    n_attn_heads = attn_sinks.shape[1]
    n_ssm_heads = ssm_D.shape[1]
    q_width = n_attn_heads * HEAD_DIM
    kv_width = (n_attn_heads // KV_GROUP) * HEAD_DIM
    hp = n_ssm_heads * HEAD_DIM
    bc = SSM_GROUPS * SSM_STATE
    assert bsz == 1 and s_len % CHUNK == 0

    bn = COL_TILE
    nq, nkv, nzx = q_width // bn, 2 * kv_width // bn, 2 * hp // bn
    assert q_width % bn == 0 and (2 * kv_width) % bn == 0 and hp % bn == 0 and bc % bn == 0

    def proj_dest(j):
        return j + jnp.where(j < nq, 0, jnp.where(j < nq + nkv, nzx, jnp.where(j < nq + nkv + nzx, -nkv, 0)))

    n_proj = q_width + 2 * kv_width + 2 * hp + 2 * bc
    p_z, p_x = q_width, q_width + hp
    p_k = p_x + hp
    p_v = p_k + kv_width
    p_b = p_v + kv_width
    p_c = p_b + bc

    xm = x.reshape(s_len, d)
    hb, rstd = _prep(xm, ffn_norm[0, 0])

    for l in range(depth):
        i = l // 2
        g_act = _pair_proj(hb, rstd, ffn_w13, (l, 0), act="swiglu", name="ffn_up")
        xm, hb, rstd = _residual_proj([g_act], ffn_w2, (l, 0), xm, mix_norm[l], alpha=0.5)
        if l % 2 == 0:
            proj = _mixer_in(hb, rstd, hy_w_in, (i,), n_cols=n_proj, dest=proj_dest)
            dt_raw = _dt_proj(hb, rstd, hy_w_in, (i,), dt_col=n_proj, n_heads=n_ssm_heads)
            attn = _attention(proj, attn_q_norm[i], attn_k_norm[i], attn_sinks[i],
                              q_width=q_width, k_col=p_k, v_col=p_v, kv_width=kv_width)
            ssm = _ssd(proj, dt_raw, ssm_conv_w[i], ssm_conv_b[i], ssm_dt_bias[i], ssm_A_log[i],
                       ssm_D[i], ssm_norm[i], z_col=p_z, xs_col=p_x, b_col=p_b, c_col=p_c, hp=hp)
            xm, hb, rstd = _residual_proj([attn, ssm], hy_w_out, (i,), xm, ffn_norm[l, 1], alpha=1.0)
        else:
            u, v = _pair_proj(hb, rstd, gm_w_in, (i,), act="gelu", name="gmlp_in")
            a = _gmlp_gate(u, v, gm_ln_g[i], gm_ln_b[i], gm_ws[i], gm_bs[i])
            xm, hb, rstd = _residual_proj([a], gm_w_out, (i,), xm, ffn_norm[l, 1], alpha=1.0)
        g_act = _pair_proj(hb, rstd, ffn_w13, (l, 1), act="swiglu", name="ffn_up")
        nxt = ffn_norm[l + 1, 0] if l + 1 < depth else None
        xm, hb, rstd = _residual_proj([g_act], ffn_w2, (l, 1), xm, nxt, alpha=0.5)
    return xm.reshape(bsz, s_len, d)
```

```python
import functools

import jax
import jax.numpy as jnp
from jax import lax
from jax.experimental import pallas as pl
from jax.experimental.pallas import tpu as pltpu

F32 = jnp.float32
BF16 = jnp.bfloat16

EPS = 1e-6
LANES = 128
HEAD_DIM = 64
WINDOW = 128
KV_GROUP = 8
SSM_GROUPS = 4
SSM_STATE = 128
SSM_CONV = 4
CHUNK = 128
GM_GROUPS = 8
NEG_BIG = -1e30
VMEM_LIMIT = 62 * 1024 * 1024
ROW_TILE = 1024
COL_TILE = 512
RES_ROW_TILE = 512
RES_COL_TILE = 1024
SUB_COLS = 256


def _cparams(n_axes):
    return pltpu.CompilerParams(dimension_semantics=("arbitrary",) * n_axes,
                                vmem_limit_bytes=VMEM_LIMIT)


def _lane_tile(v, reps):
    return v if reps == 1 else jnp.concatenate([v] * reps, axis=1)


def _dot(a, b):
    return jnp.dot(a, b, preferred_element_type=F32)


def _dot_nt(a, b):
    return lax.dot_general(a, b, (((1,), (1,)), ((), ())), preferred_element_type=F32)


def _split2(v):
    hi = v.astype(BF16)
    lo = (v - hi.astype(F32)).astype(BF16)
    return hi, lo


def _split3(v):
    hi = v.astype(BF16)
    r = v - hi.astype(F32)
    mid = r.astype(BF16)
    lo = (r - mid.astype(F32)).astype(BF16)
    return hi, mid, lo


def _sigmoid(v):
    return 1.0 / (1.0 + jnp.exp(-v))


def _silu(v):
    return v * _sigmoid(v)


def _gelu_tanh(v):
    c = 0.7978845608028654
    return 0.5 * v * (1.0 + jnp.tanh(c * (v + 0.044715 * (v * v * v))))


def _row_sumsq_lanes(v):
    n = v.shape[1] // LANES
    acc = v[:, :LANES] * v[:, :LANES]
    for t in range(1, n):
        s = v[:, t * LANES:(t + 1) * LANES]
        acc = acc + s * s
    return acc


def _prep_kernel(x_ref, g_ref, hb_ref, rstd_ref, *, d_model):
    x = x_ref[...]
    hb_ref[...] = (x * g_ref[...]).astype(BF16)
    ssq = jnp.sum(x * x, axis=-1, keepdims=True)
    rstd_ref[...] = jnp.broadcast_to(lax.rsqrt(ssq / d_model + EPS), rstd_ref.shape)


def _prep(x, gain):
    m, d = x.shape
    bm = min(256, m)
    return pl.pallas_call(
        functools.partial(_prep_kernel, d_model=d),
        grid=(m // bm,),
        in_specs=[pl.BlockSpec((bm, d), lambda i: (i, 0)),
                  pl.BlockSpec((1, d), lambda i: (0, 0))],
        out_specs=[pl.BlockSpec((bm, d), lambda i: (i, 0)),
                   pl.BlockSpec((bm, LANES), lambda i: (i, 0))],
        out_shape=[jax.ShapeDtypeStruct((m, d), BF16),
                   jax.ShapeDtypeStruct((m, LANES), F32)],
        compiler_params=_cparams(1),
        name="prep_norm",
    )(x, gain.reshape(1, d))


def _stream_step(w_chunk_refs, wbuf_ref, compute):
    jj = pl.program_id(0)
    i = pl.program_id(1)
    kc = w_chunk_refs[0].shape[0]

    def cast(slot):
        row0 = pl.multiple_of(i * kc, kc)
        for w, ref in enumerate(w_chunk_refs):
            wbuf_ref[slot, w, pl.ds(row0, kc), :] = ref[...].astype(BF16)

    @pl.when(jj == 0)
    def _():
        cast(0)

    for parity in (0, 1):
        @pl.when((jj > 0) & (jj % 2 == parity))
        def _():
            compute([wbuf_ref.at[1 - parity, w] for w in range(len(w_chunk_refs))])
            cast(parity)


def _stream_maps(ni, nj, lead, col_offsets, dest=None):
    def act_idx(jj, i):
        return (jnp.where(jj == 0, 0, i), 0)

    def out_idx(jj, i):
        j = jnp.maximum(jj - 1, 0)
        return (jnp.where(jj == 0, 0, i), j if dest is None else dest(j))

    def w_idx(off):
        return lambda jj, i: lead + (jnp.where(jj == nj, ni - 1, i), jnp.minimum(jj, nj - 1) + off)

    return act_idx, out_idx, [w_idx(off) for off in col_offsets]


def _tiles(m, k, row_tile=None):
    bm = min(row_tile or ROW_TILE, m)
    ni = m // bm
    assert m % bm == 0 and k % ni == 0 and (k // ni) % 16 == 0
    return bm, ni, k // ni


def _pair_kernel(hb_ref, rstd_ref, wa_ref, wb_ref, *rest, act):
    out_refs, wbuf_ref = rest[:-1], rest[-1]

    def compute(w):
        hb = hb_ref[...]
        bn = out_refs[0].shape[1]
        sub = min(SUB_COLS, bn)
        r = _lane_tile(rstd_ref[...], sub // LANES)
        for c in range(0, bn, sub):
            a = _dot(hb, w[0][:, c:c + sub]) * r
            b = _dot(hb, w[1][:, c:c + sub]) * r
            if act == "swiglu":
                out_refs[0][:, c:c + sub] = (_silu(a) * b).astype(out_refs[0].dtype)
            else:
                out_refs[0][:, c:c + sub] = _gelu_tanh(a).astype(out_refs[0].dtype)
                out_refs[1][:, c:c + sub] = _gelu_tanh(b).astype(out_refs[1].dtype)

    _stream_step([wa_ref, wb_ref], wbuf_ref, compute)


def _pair_proj(hb, rstd, w_all, lead, *, act, name):
    m, d = hb.shape
    width = w_all.shape[-1] // 2
    bn = COL_TILE
    bm, ni, kc = _tiles(m, d)
    nj = width // bn
    act_idx, out_idx, (wa_idx, wb_idx) = _stream_maps(ni, nj, lead, [0, nj])
    n_out = 1 if act == "swiglu" else 2
    wspec = lambda idx: pl.BlockSpec((None,) * len(lead) + (kc, bn), idx)
    outs = pl.pallas_call(
        functools.partial(_pair_kernel, act=act),
        grid=(nj + 1, ni),
        in_specs=[pl.BlockSpec((bm, d), act_idx), pl.BlockSpec((bm, LANES), act_idx),
                  wspec(wa_idx), wspec(wb_idx)],
        out_specs=[pl.BlockSpec((bm, bn), out_idx)] * n_out,
        out_shape=[jax.ShapeDtypeStruct((m, width), BF16)] * n_out,
        scratch_shapes=[pltpu.VMEM((2, 2, d, bn), BF16)],
        compiler_params=_cparams(2),
        name=name,
    )(hb, rstd, w_all, w_all)
    return outs[0] if n_out == 1 else outs


def _mixer_in_kernel(hb_ref, rstd_ref, wt_ref, proj_ref, wbuf_ref):
    def compute(w):
        hb = hb_ref[...]
        bn = proj_ref.shape[1]
        sub = min(SUB_COLS, bn)
        r = _lane_tile(rstd_ref[...], sub // LANES)
        for c in range(0, bn, sub):
            proj_ref[:, c:c + sub] = (_dot_nt(hb, w[0][c:c + sub, :]) * r).astype(proj_ref.dtype)

    _stream_step([wt_ref], wbuf_ref, compute)


def _mixer_in(hb, rstd, wt_all, lead, *, n_cols, dest):
    m, d = hb.shape
    bn = COL_TILE
    bm, ni, _ = _tiles(m, d)
    rc = bn // ni
    assert bn % ni == 0 and rc % 16 == 0
    nj = n_cols // bn
    act_idx, out_idx, _ = _stream_maps(ni, nj, lead, [0], dest)

    def wt_idx(jj, i):
        return lead + (jnp.minimum(jj, nj - 1) * ni + jnp.where(jj == nj, ni - 1, i), 0)

    return pl.pallas_call(
        _mixer_in_kernel,
        grid=(nj + 1, ni),
        in_specs=[pl.BlockSpec((bm, d), act_idx), pl.BlockSpec((bm, LANES), act_idx),
                  pl.BlockSpec((None,) * len(lead) + (rc, d), wt_idx)],
        out_specs=pl.BlockSpec((bm, bn), out_idx),
        out_shape=jax.ShapeDtypeStruct((m, n_cols), BF16),
        scratch_shapes=[pltpu.VMEM((2, 1, bn, d), BF16)],
        compiler_params=_cparams(2),
        name="mixer_in",
    )(hb, rstd, wt_all)


def _residual_kernel(*refs, n_a, alpha, with_norm, d_model):
    a_refs = refs[:n_a]
    w_ref, x_ref = refs[n_a], refs[n_a + 1]
    if with_norm:
        g_ref, xo_ref, hb_ref, rstd_ref, wbuf_ref, ssq_ref = refs[n_a + 2:]
    else:
        xo_ref, wbuf_ref = refs[n_a + 2:]
    jj = pl.program_id(0)
    i = pl.program_id(1)

    def compute(w):
        ka = a_refs[0].shape[1]
        bm, bn = xo_ref.shape
        sub = min(SUB_COLS, bn)
        a_vals = [a_ref[...] for a_ref in a_refs]
        part = None
        for c in range(0, bn, sub):
            acc = _dot(a_vals[0], w[0][0:ka, c:c + sub])
            for k in range(1, n_a):
                acc = acc + _dot(a_vals[k], w[0][k * ka:(k + 1) * ka, c:c + sub])
            xn = x_ref[:, c:c + sub] + (acc if alpha == 1.0 else alpha * acc)
            xo_ref[:, c:c + sub] = xn
            if with_norm:
                hb_ref[:, c:c + sub] = (xn * g_ref[:, c:c + sub]).astype(hb_ref.dtype)
                sq = _row_sumsq_lanes(xn)
                part = sq if part is None else part + sq
        if not with_norm:
            return
        rows = pl.ds(pl.multiple_of(i * bm, bm), bm)

        @pl.when(jj == 1)
        def _():
            ssq_ref[rows, :] = part

        @pl.when(jj > 1)
        def _():
            ssq_ref[rows, :] += part

        @pl.when(jj == pl.num_programs(0) - 1)
        def _():
            ssq = jnp.sum(ssq_ref[rows, :], axis=-1, keepdims=True)
            rstd_ref[...] = jnp.broadcast_to(lax.rsqrt(ssq / d_model + EPS), rstd_ref.shape)

    _stream_step([w_ref], wbuf_ref, compute)


def _residual_proj(a_list, w_all, lead, x, gain_next, *, alpha):
    m, d = x.shape
    n_a = len(a_list)
    ka = a_list[0].shape[1]
    k_total = ka * n_a
    assert all(a.shape[1] == ka for a in a_list) and k_total == w_all.shape[-2]
    bn = min(RES_COL_TILE, d)
    bm, ni, kc = _tiles(m, k_total, RES_ROW_TILE)
    nj = d // bn
    with_norm = gain_next is not None
    act_idx, out_idx, (w_idx,) = _stream_maps(ni, nj, lead, [0])
    in_specs = [pl.BlockSpec((bm, ka), act_idx) for _ in range(n_a)]
    in_specs += [pl.BlockSpec((None,) * len(lead) + (kc, bn), w_idx), pl.BlockSpec((bm, bn), out_idx)]
    operands = list(a_list) + [w_all, x]
    out_specs = [pl.BlockSpec((bm, bn), out_idx)]
    out_shape = [jax.ShapeDtypeStruct((m, d), F32)]
    scratch = [pltpu.VMEM((2, 1, k_total, bn), BF16)]
    if with_norm:
        in_specs += [pl.BlockSpec((1, bn), lambda jj, i: (0, jnp.maximum(jj - 1, 0)))]
        operands += [gain_next.reshape(1, d)]
        out_specs += [pl.BlockSpec((bm, bn), out_idx),
                      pl.BlockSpec((bm, LANES), lambda jj, i: (jnp.where(jj == nj, i, 0), 0))]
        out_shape += [jax.ShapeDtypeStruct((m, d), BF16),
                      jax.ShapeDtypeStruct((m, LANES), F32)]
        scratch += [pltpu.VMEM((m, LANES), F32)]
    outs = pl.pallas_call(
        functools.partial(_residual_kernel, n_a=n_a, alpha=alpha, with_norm=with_norm, d_model=d),
        grid=(nj + 1, ni),
        in_specs=in_specs,
        out_specs=out_specs,
        out_shape=out_shape,
        scratch_shapes=scratch,
        compiler_params=_cparams(2),
        name="residual_proj",
    )(*operands)
    if with_norm:
        return outs[0], outs[1], outs[2]
    return outs[0], None, None


def _dt_kernel(hb_ref, rstd_ref, wt_ref, dt_ref, *, n_heads):
    row = lax.broadcasted_iota(jnp.int32, wt_ref.shape, 0)
    wt = jnp.where(row < n_heads, wt_ref[...], 0.0).astype(BF16)
    dt_ref[...] = _dot_nt(hb_ref[...], wt) * rstd_ref[...]


def _dt_proj(hb, rstd, wt_all, lead, *, dt_col, n_heads):
    m, d = hb.shape
    bm = min(ROW_TILE, m)
    assert dt_col % LANES == 0
    return pl.pallas_call(
        functools.partial(_dt_kernel, n_heads=n_heads),
        grid=(m // bm,),
        in_specs=[pl.BlockSpec((bm, d), lambda i: (i, 0)),
                  pl.BlockSpec((bm, LANES), lambda i: (i, 0)),
                  pl.BlockSpec((None,) * len(lead) + (LANES, d), lambda i: lead + (dt_col // LANES, 0))],
        out_specs=pl.BlockSpec((bm, LANES), lambda i: (i, 0)),
        out_shape=jax.ShapeDtypeStruct((m, LANES), F32),
        compiler_params=_cparams(1),
        name="dt_proj",
    )(hb, rstd, wt_all)


def _head_ones():
    r = lax.broadcasted_iota(jnp.int32, (LANES, LANES), 0) // HEAD_DIM
    c = lax.broadcasted_iota(jnp.int32, (LANES, LANES), 1) // HEAD_DIM
    return (r == c).astype(BF16)


def _head_rms(v, gain_row):
    rows, width = v.shape
    nt = width // LANES
    ones = _head_ones()
    sq = v * v
    stacked = jnp.concatenate([sq[:, t * LANES:(t + 1) * LANES] for t in range(nt)], axis=0)
    hi, lo = _split2(stacked)
    ssq = _dot(hi, ones) + _dot(lo, ones)
    rs = lax.rsqrt(ssq / HEAD_DIM + EPS)
    return [v[:, t * LANES:(t + 1) * LANES] * rs[t * rows:(t + 1) * rows] * gain_row for t in range(nt)]


def _attn_kernel(q_ref, kp_ref, kc_ref, vp_ref, vc_ref, qg_ref, kg_ref, sink_ref, o_ref):
    n = pl.program_id(0)
    w = WINDOW
    n_kv = kc_ref.shape[1] // HEAD_DIM
    n_tiles = q_ref.shape[1] // LANES
    hd = HEAD_DIM

    qg = qg_ref[...]
    qn = []
    for t in range(n_tiles):
        qt = q_ref[:, t * LANES:(t + 1) * LANES].astype(F32).T
        for half in range(2):
            blk = qt[half * hd:(half + 1) * hd, :]
            rs = lax.rsqrt(jnp.sum(blk * blk, axis=0, keepdims=True) / hd + EPS)
            qn.append((blk * rs * qg).astype(BF16))

    kk = jnp.concatenate([kp_ref[...], kc_ref[...]], axis=0).astype(F32)
    k_tiles = [kt.astype(BF16) for kt in _head_rms(kk, kg_ref[...])]
    vv = jnp.concatenate([vp_ref[...], vc_ref[...]], axis=0).astype(F32)
    vt_tiles = [vv[:, a * LANES:(a + 1) * LANES].T.astype(BF16) for a in range(n_kv // 2)]

    kj = lax.broadcasted_iota(jnp.int32, (2 * w, w), 0)
    qi = lax.broadcasted_iota(jnp.int32, (2 * w, w), 1)
    first_key = jnp.where(n > 0, 0, w)
    valid = (kj > qi) & (kj <= qi + w) & (kj >= first_key)
    zeros = jnp.zeros((hd, w), BF16)

    for h in range(n_kv):
        a, half = h // 2, h % 2
        cols = []
        for g in range(KV_GROUP):
            qh = qn[h * KV_GROUP + g]
            cols.append(jnp.concatenate([qh, zeros] if half == 0 else [zeros, qh], axis=0))
        st = _dot(k_tiles[a], jnp.concatenate(cols, axis=1))
        p_cols, inv_cols = [], []
        for g in range(KV_GROUP):
            s = jnp.where(valid, st[:, g * w:(g + 1) * w], NEG_BIG)
            sk = sink_ref[h:h + 1, g * w:(g + 1) * w]
            mx = jnp.maximum(jnp.max(s, axis=0, keepdims=True), sk)
            p = jnp.exp(s - mx)
            den = jnp.sum(p, axis=0, keepdims=True) + jnp.exp(sk - mx)
            p_cols.append(p.astype(BF16))
            inv_cols.append(1.0 / den)
        vt = vt_tiles[a][half * hd:(half + 1) * hd, :]
        ot = _dot(vt, jnp.concatenate(p_cols, axis=1)) * jnp.concatenate(inv_cols, axis=1)
        for pr in range(KV_GROUP // 2):
            pair = jnp.concatenate([ot[:, (2 * pr) * w:(2 * pr + 1) * w],
                                    ot[:, (2 * pr + 1) * w:(2 * pr + 2) * w]], axis=0)
            t = h * (KV_GROUP // 2) + pr
            o_ref[:, t * LANES:(t + 1) * LANES] = pair.T.astype(o_ref.dtype)


def _attention(proj, q_gain, k_gain, sinks, *, q_width, k_col, v_col, kv_width):
    s_len = proj.shape[0]
    nb = s_len // WINDOW
    n_kv = kv_width // HEAD_DIM
    kb, vb = k_col // kv_width, v_col // kv_width
    qg = jnp.broadcast_to((q_gain.astype(F32) * HEAD_DIM ** -0.5)[:, None], (HEAD_DIM, WINDOW))
    kg = jnp.concatenate([k_gain, k_gain]).reshape(1, LANES).astype(F32)
    sink_rows = jnp.repeat(sinks.astype(F32), WINDOW).reshape(n_kv, KV_GROUP * WINDOW)
    prev = lambda n: jnp.maximum(n - 1, 0)
    const = lambda shape: pl.BlockSpec(shape, lambda n: (0, 0))
    return pl.pallas_call(
        _attn_kernel,
        grid=(nb,),
        in_specs=[pl.BlockSpec((WINDOW, q_width), lambda n: (n, 0)),
                  pl.BlockSpec((WINDOW, kv_width), lambda n: (prev(n), kb)),
                  pl.BlockSpec((WINDOW, kv_width), lambda n: (n, kb)),
                  pl.BlockSpec((WINDOW, kv_width), lambda n: (prev(n), vb)),
                  pl.BlockSpec((WINDOW, kv_width), lambda n: (n, vb)),
                  const((HEAD_DIM, WINDOW)), const((1, LANES)), const((n_kv, KV_GROUP * WINDOW))],
        out_specs=pl.BlockSpec((WINDOW, q_width), lambda n: (n, 0)),
        out_shape=jax.ShapeDtypeStruct((s_len, q_width), BF16),
        compiler_params=_cparams(1),
        name="swa_attention",
    )(proj, proj, proj, proj, proj, qg, kg, sink_rows)


def _ssd_kernel(z_ref, xs_ref, b_ref, c_ref, dt_ref, cwx_ref, cwb_ref, cwc_ref, cbx_ref, cbb_ref, cbc_ref,
                dtb_ref, a_ref, dsk_ref, ng_ref, o_ref,
                px_ref, pb_ref, pc_ref, state_ref):
    c_idx = pl.program_id(0)
    L = CHUNK
    hp = xs_ref.shape[1]
    n_heads = hp // HEAD_DIM
    gw = hp // SSM_GROUPS
    hist = 8

    @pl.when(c_idx == 0)
    def _():
        px_ref[0:hist, :] = jnp.zeros((hist, px_ref.shape[1]), F32)
        pb_ref[0:hist, :] = jnp.zeros((hist, pb_ref.shape[1]), F32)
        pc_ref[0:hist, :] = jnp.zeros((hist, pc_ref.shape[1]), F32)
        state_ref[...] = jnp.zeros(state_ref.shape, F32)

    def conv_silu(src_ref, pad_ref, w_ref, bias_ref):
        pad_ref[hist:hist + L, :] = src_ref[...].astype(F32)
        acc = bias_ref[...]
        for k in range(SSM_CONV):
            acc = acc + pad_ref[pl.ds(hist - (SSM_CONV - 1) + k, L), :] * w_ref[k:k + 1, :]
        pad_ref[0:hist, :] = pad_ref[L:L + hist, :]
        return _silu(acc)

    xs = conv_silu(xs_ref, px_ref, cwx_ref, cbx_ref)
    bm = conv_silu(b_ref, pb_ref, cwb_ref, cbb_ref)
    cm = conv_silu(c_ref, pc_ref, cwc_ref, cbc_ref)

    dt_in = dt_ref[...] + dtb_ref[...]
    dt = jnp.maximum(dt_in, 0.0) + jnp.log1p(jnp.exp(-jnp.abs(dt_in)))
    da = dt * a_ref[...]
    ri = lax.broadcasted_iota(jnp.int32, (L, L), 0)
    ci = lax.broadcasted_iota(jnp.int32, (L, L), 1)
    causal = ci <= ri
    tril = causal.astype(BF16)
    p0, p1, p2 = _split3(da)
    a_cs = _dot(tril, p0) + _dot(tril, p1) + _dot(tril, p2)
    a_cs_t = a_cs.T
    a_last = a_cs[L - 1:L, :]
    decay_end = jnp.exp(a_last - a_cs)
    e_acs = jnp.exp(a_cs)

    er = lax.broadcasted_iota(jnp.int32, (LANES, hp), 0)
    ec = lax.broadcasted_iota(jnp.int32, (LANES, hp), 1) // HEAD_DIM
    expand = (er == ec).astype(BF16)

    def expand_heads(v):
        hi, lo = _split2(v)
        return _dot(hi, expand) + _dot(lo, expand)

    dt_e = expand_heads(dt)
    xdt = xs * dt_e
    x_end = (xdt * expand_heads(decay_end)).astype(BF16)
    e_acs_e = expand_heads(e_acs)
    chunk_decay_e = e_acs_e[L - 1:L, :]
    xdt_b = xdt.astype(BF16)

    lane2 = lax.broadcasted_iota(jnp.int32, (L, LANES), 1)
    low_half = lane2 < HEAD_DIM
    y_tiles = []
    for g in range(SSM_GROUPS):
        bg = bm[:, g * SSM_STATE:(g + 1) * SSM_STATE]
        cg = cm[:, g * SSM_STATE:(g + 1) * SSM_STATE].astype(BF16)
        cb = _dot_nt(cg, bg.astype(BF16))
        heads_per_group = n_heads // SSM_GROUPS
        for pr in range(heads_per_group // 2):
            h0 = g * heads_per_group + 2 * pr
            t = h0 // 2
            ms = []
            for h in (h0, h0 + 1):
                seg = a_cs[:, h:h + 1] - a_cs_t[h:h + 1, :]
                ms.append((cb * jnp.exp(jnp.where(causal, seg, NEG_BIG))).astype(BF16))
            xt = xdt_b[:, t * LANES:(t + 1) * LANES]
            zero = jnp.zeros_like(xt)
            rhs = jnp.concatenate([jnp.where(low_half, xt, zero), jnp.where(low_half, zero, xt)], axis=0)
            y_tiles.append(_dot(jnp.concatenate(ms, axis=1), rhs))
    y_diag = jnp.concatenate(y_tiles, axis=1)

    y_off_tiles, new_state = [], []
    for g in range(SSM_GROUPS):
        sl = slice(g * gw, (g + 1) * gw)
        st = state_ref[:, sl]
        cg = cm[:, g * SSM_STATE:(g + 1) * SSM_STATE].astype(BF16)
        y_off_tiles.append(_dot(cg, st.astype(BF16)))
        bg_t = bm[:, g * SSM_STATE:(g + 1) * SSM_STATE].T.astype(BF16)
        new_state.append(st * chunk_decay_e[:, sl] + _dot(bg_t, x_end[:, sl]))
    y_off = jnp.concatenate(y_off_tiles, axis=1) * e_acs_e
    for g in range(SSM_GROUPS):
        state_ref[:, g * gw:(g + 1) * gw] = new_state[g]

    y = y_diag + y_off + xs * dsk_ref[...]
    gated = y * _silu(z_ref[...].astype(F32))
    outs = []
    for g in range(SSM_GROUPS):
        gg = gated[:, g * gw:(g + 1) * gw]
        ms_ = jnp.sum(gg * gg, axis=-1, keepdims=True) / gw
        outs.append(gg * lax.rsqrt(ms_ + EPS))
    o_ref[...] = (jnp.concatenate(outs, axis=1) * ng_ref[...]).astype(o_ref.dtype)


def _ssd(proj, dt_raw, conv_w, conv_b, dt_bias, a_log, d_skip, norm_g, *, z_col, xs_col, b_col, c_col, hp):
    s_len = proj.shape[0]
    nc = s_len // CHUNK
    n_heads = hp // HEAD_DIM
    bc = SSM_GROUPS * SSM_STATE
    row = lambda v: v.reshape(1, -1).astype(F32)
    pad_heads = lambda v: jnp.pad(v.astype(F32), (0, LANES - n_heads)).reshape(1, LANES)
    cw_x, cw_b, cw_c = conv_w[:, :hp], conv_w[:, hp:hp + bc], conv_w[:, hp + bc:]
    cb_x, cb_b, cb_c = row(conv_b[:hp]), row(conv_b[hp:hp + bc]), row(conv_b[hp + bc:])
    a_neg = pad_heads(-jnp.exp(a_log.astype(F32)))
    const = lambda shape: pl.BlockSpec(shape, lambda c: (0, 0))
    return pl.pallas_call(
        _ssd_kernel,
        grid=(nc,),
        in_specs=[pl.BlockSpec((CHUNK, hp), lambda c: (c, z_col // hp)),
                  pl.BlockSpec((CHUNK, hp), lambda c: (c, xs_col // hp)),
                  pl.BlockSpec((CHUNK, bc), lambda c: (c, b_col // bc)),
                  pl.BlockSpec((CHUNK, bc), lambda c: (c, c_col // bc)),
                  pl.BlockSpec((CHUNK, LANES), lambda c: (c, 0)),
                  const((SSM_CONV, hp)), const((SSM_CONV, bc)), const((SSM_CONV, bc)),
                  const((1, hp)), const((1, bc)), const((1, bc)),
                  const((1, LANES)), const((1, LANES)), const((1, hp)), const((1, hp))],
        out_specs=pl.BlockSpec((CHUNK, hp), lambda c: (c, 0)),
        out_shape=jax.ShapeDtypeStruct((s_len, hp), BF16),
        scratch_shapes=[pltpu.VMEM((CHUNK + 8, hp), F32),
                        pltpu.VMEM((CHUNK + 8, bc), F32),
                        pltpu.VMEM((CHUNK + 8, bc), F32),
                        pltpu.VMEM((SSM_STATE, hp), F32)],
        compiler_params=_cparams(1),
        name="ssd_mixer",
    )(proj, proj, proj, proj, dt_raw,
      cw_x.astype(F32), cw_b.astype(F32), cw_c.astype(F32), cb_x, cb_b, cb_c,
      pad_heads(dt_bias), a_neg, row(jnp.repeat(d_skip, HEAD_DIM)), row(norm_g))


def _gate_kernel(u_ref, v_ref, lg_ref, lb_ref, ws_ref, bst_ref, o_ref):
    L = CHUNK
    width = u_ref.shape[1]
    gw = width // GM_GROUPS
    groups = [slice(g * gw, (g + 1) * gw) for g in range(GM_GROUPS)]
    part = v_ref[:, groups[0]].astype(F32)
    for sl in groups[1:]:
        part = part + v_ref[:, sl].astype(F32)
    mean = jnp.sum(part, axis=-1, keepdims=True) / width
    sq = None
    for sl in groups:
        dvn = v_ref[:, sl].astype(F32) - mean
        sq = dvn * dvn if sq is None else sq + dvn * dvn
    rs = lax.rsqrt(jnp.sum(sq, axis=-1, keepdims=True) / width + EPS)
    ri = lax.broadcasted_iota(jnp.int32, (L, L), 0)
    ci = lax.broadcasted_iota(jnp.int32, (L, L), 1)
    causal = ci <= ri
    for g, sl in enumerate(groups):
        vn = (v_ref[:, sl].astype(F32) - mean) * rs * lg_ref[:, sl] + lb_ref[:, sl]
        wc = jnp.where(causal, ws_ref[g], 0.0).astype(BF16)
        sv = _dot(wc, vn.astype(BF16)) + bst_ref[:, g:g + 1]
        o_ref[:, sl] = (u_ref[:, sl].astype(F32) * sv).astype(o_ref.dtype)


def _gmlp_gate(u, v, ln_g, ln_b, ws, bs):
    s_len, width = u.shape
    nc = s_len // CHUNK
    row = lambda t: t.reshape(1, width).astype(F32)
    return pl.pallas_call(
        _gate_kernel,
        grid=(nc,),
        in_specs=[pl.BlockSpec((CHUNK, width), lambda c: (c, 0)),
                  pl.BlockSpec((CHUNK, width), lambda c: (c, 0)),
                  pl.BlockSpec((1, width), lambda c: (0, 0)),
                  pl.BlockSpec((1, width), lambda c: (0, 0)),
                  pl.BlockSpec((GM_GROUPS, CHUNK, CHUNK), lambda c: (0, 0, 0)),
                  pl.BlockSpec((CHUNK, GM_GROUPS), lambda c: (0, 0))],
        out_specs=pl.BlockSpec((CHUNK, width), lambda c: (c, 0)),
        out_shape=jax.ShapeDtypeStruct((s_len, width), BF16),
        compiler_params=_cparams(1),
        name="gmlp_gate",
    )(u, v, row(ln_g), row(ln_b), ws.astype(F32), bs.T.astype(F32))


def kernel(x, ffn_norm, ffn_w13, ffn_w2, mix_norm, hy_w_in, attn_q_norm, attn_k_norm, attn_sinks,
           ssm_conv_w, ssm_conv_b, ssm_dt_bias, ssm_A_log, ssm_D, ssm_norm, hy_w_out,
           gm_w_in, gm_ln_g, gm_ln_b, gm_ws, gm_bs, gm_w_out):
    bsz, s_len, d = x.shape
    depth = ffn_norm.shape[0]
    n_attn_heads = attn_sinks.shape[1]
    n_ssm_heads = ssm_D.shape[1]
    q_width = n_attn_heads * HEAD_DIM
    kv_width = (n_attn_heads // KV_GROUP) * HEAD_DIM
    hp = n_ssm_heads * HEAD_DIM
    bc = SSM_GROUPS * SSM_STATE
    assert bsz == 1 and s_len % CHUNK == 0

    bn = COL_TILE
    nq, nkv, nzx = q_width // bn, 2 * kv_width // bn, 2 * hp // bn
    assert q_width % bn == 0 and (2 * kv_width) % bn == 0 and hp % bn == 0 and bc % bn == 0

    def proj_dest(j):
        return j + jnp.where(j < nq, 0, jnp.where(j < nq + nkv, nzx, jnp.where(j < nq + nkv + nzx, -nkv, 0)))

    n_proj = q_width + 2 * kv_width + 2 * hp + 2 * bc
    p_z, p_x = q_width, q_width + hp
    p_k = p_x + hp
    p_v = p_k + kv_width
    p_b = p_v + kv_width
    p_c = p_b + bc

    w_in_t = jnp.swapaxes(hy_w_in, 1, 2)

    xm = x.reshape(s_len, d)
    hb, rstd = _prep(xm, ffn_norm[0, 0])

    for l in range(depth):
        i = l // 2
        g_act = _pair_proj(hb, rstd, ffn_w13, (l, 0), act="swiglu", name="ffn_up")
        xm, hb, rstd = _residual_proj([g_act], ffn_w2, (l, 0), xm, mix_norm[l], alpha=0.5)
        if l % 2 == 0:
            proj = _mixer_in(hb, rstd, w_in_t, (i,), n_cols=n_proj, dest=proj_dest)
            dt_raw = _dt_proj(hb, rstd, w_in_t, (i,), dt_col=n_proj, n_heads=n_ssm_heads)
            attn = _attention(proj, attn_q_norm[i], attn_k_norm[i], attn_sinks[i],
                              q_width=q_width, k_col=p_k, v_col=p_v, kv_width=kv_width)
            ssm = _ssd(proj, dt_raw, ssm_conv_w[i], ssm_conv_b[i], ssm_dt_bias[i], ssm_A_log[i],
                       ssm_D[i], ssm_norm[i], z_col=p_z, xs_col=p_x, b_col=p_b, c_col=p_c, hp=hp)
            xm, hb, rstd = _residual_proj([attn, ssm], hy_w_out, (i,), xm, ffn_norm[l, 1], alpha=1.0)
        else:
            u, v = _pair_proj(hb, rstd, gm_w_in, (i,), act="gelu", name="gmlp_in")
            a = _gmlp_gate(u, v, gm_ln_g[i], gm_ln_b[i], gm_ws[i], gm_bs[i])
            xm, hb, rstd = _residual_proj([a], gm_w_out, (i,), xm, ffn_norm[l, 1], alpha=1.0)
        g_act = _pair_proj(hb, rstd, ffn_w13, (l, 1), act="swiglu", name="ffn_up")
        nxt = ffn_norm[l + 1, 0] if l + 1 < depth else None
        xm, hb, rstd = _residual_proj([g_act], ffn_w2, (l, 1), xm, nxt, alpha=0.5)
    return xm.reshape(bsz, s_len, d)
```

```python
import functools

import jax
import jax.numpy as jnp
from jax import lax
from jax.experimental import pallas as pl
from jax.experimental.pallas import tpu as pltpu

F32 = jnp.float32
BF16 = jnp.bfloat16

EPS = 1e-6
LANES = 128
BF16_SUBLANES = 16
HEAD_DIM = 64
WINDOW = 128
KV_GROUP = 8
SSM_GROUPS = 4
SSM_STATE = 128
SSM_CONV = 4
CHUNK = 128
GM_GROUPS = 8
NEG_BIG = -1e30
VMEM_LIMIT = 62 * 1024 * 1024
ROW_TILE = 1024
COL_TILE = 512
RES_ROW_TILE = 512
RES_COL_TILE = 1024
SUB_COLS = 256


def _cparams(n_axes):
    return pltpu.CompilerParams(dimension_semantics=("arbitrary",) * n_axes,
                                vmem_limit_bytes=VMEM_LIMIT)


def _lane_tile(v, reps):
    return v if reps == 1 else jnp.concatenate([v] * reps, axis=1)


def _dot(a, b):
    return jnp.dot(a, b, preferred_element_type=F32)


def _dot_nt(a, b):
    return lax.dot_general(a, b, (((1,), (1,)), ((), ())), preferred_element_type=F32)


def _split2(v):
    hi = v.astype(BF16)
    lo = (v - hi.astype(F32)).astype(BF16)
    return hi, lo


def _split3(v):
    hi = v.astype(BF16)
    r = v - hi.astype(F32)
    mid = r.astype(BF16)
    lo = (r - mid.astype(F32)).astype(BF16)
    return hi, mid, lo


def _sigmoid(v):
    return 1.0 / (1.0 + jnp.exp(-v))


def _silu(v):
    return v * _sigmoid(v)


def _gelu_tanh(v):
    c = 0.7978845608028654
    return 0.5 * v * (1.0 + jnp.tanh(c * (v + 0.044715 * (v * v * v))))


def _row_sumsq_lanes(v):
    n = v.shape[1] // LANES
    acc = v[:, :LANES] * v[:, :LANES]
    for t in range(1, n):
        s = v[:, t * LANES:(t + 1) * LANES]
        acc = acc + s * s
    return acc


def _prep_kernel(x_ref, g_ref, hb_ref, rstd_ref, *, d_model):
    x = x_ref[...]
    hb_ref[...] = (x * g_ref[...]).astype(BF16)
    ssq = jnp.sum(x * x, axis=-1, keepdims=True)
    rstd_ref[...] = jnp.broadcast_to(lax.rsqrt(ssq / d_model + EPS), rstd_ref.shape)


def _prep(x, gain):
    m, d = x.shape
    bm = min(256, m)
    return pl.pallas_call(
        functools.partial(_prep_kernel, d_model=d),
        grid=(m // bm,),
        in_specs=[pl.BlockSpec((bm, d), lambda i: (i, 0)),
                  pl.BlockSpec((1, d), lambda i: (0, 0))],
        out_specs=[pl.BlockSpec((bm, d), lambda i: (i, 0)),
                   pl.BlockSpec((bm, LANES), lambda i: (i, 0))],
        out_shape=[jax.ShapeDtypeStruct((m, d), BF16),
                   jax.ShapeDtypeStruct((m, LANES), F32)],
        compiler_params=_cparams(1),
        name="prep_norm",
    )(x, gain.reshape(1, d))


def _stream_step(w_chunk_refs, wbuf_ref, compute):
    jj = pl.program_id(0)
    i = pl.program_id(1)
    kc = w_chunk_refs[0].shape[0]

    def cast(slot):
        row0 = pl.multiple_of(i * kc, kc)
        for w, ref in enumerate(w_chunk_refs):
            wbuf_ref[slot, w, pl.ds(row0, kc), :] = ref[...].astype(BF16)

    @pl.when(jj == 0)
    def _():
        cast(0)

    for parity in (0, 1):
        @pl.when((jj > 0) & (jj % 2 == parity))
        def _():
            compute([wbuf_ref.at[1 - parity, w] for w in range(len(w_chunk_refs))])
            cast(parity)


def _stream_maps(ni, nj, lead, col_offsets, dest=None):
    def act_idx(jj, i):
        return (jnp.where(jj == 0, 0, i), 0)

    def out_idx(jj, i):
        j = jnp.maximum(jj - 1, 0)
        return (jnp.where(jj == 0, 0, i), j if dest is None else dest(j))

    def w_idx(off):
        return lambda jj, i: lead + (jnp.where(jj == nj, ni - 1, i), jnp.minimum(jj, nj - 1) + off)

    return act_idx, out_idx, [w_idx(off) for off in col_offsets]


def _tiles(m, k, row_tile=None):
    bm = min(row_tile or ROW_TILE, m)
    ni = m // bm
    assert m % bm == 0 and k % ni == 0 and (k // ni) % 16 == 0
    return bm, ni, k // ni


def _pair_kernel(hb_ref, rstd_ref, wa_ref, wb_ref, *rest, act):
    out_refs, wbuf_ref = rest[:-1], rest[-1]

    def compute(w):
        hb = hb_ref[...]
        bn = out_refs[0].shape[1]
        sub = min(SUB_COLS, bn)
        r = _lane_tile(rstd_ref[...], sub // LANES)
        for c in range(0, bn, sub):
            a = _dot(hb, w[0][:, c:c + sub]) * r
            b = _dot(hb, w[1][:, c:c + sub]) * r
            if act == "swiglu":
                out_refs[0][:, c:c + sub] = (_silu(a) * b).astype(out_refs[0].dtype)
            else:
                out_refs[0][:, c:c + sub] = _gelu_tanh(a).astype(out_refs[0].dtype)
                out_refs[1][:, c:c + sub] = _gelu_tanh(b).astype(out_refs[1].dtype)

    _stream_step([wa_ref, wb_ref], wbuf_ref, compute)


def _pair_proj(hb, rstd, w_all, lead, *, act, name):
    m, d = hb.shape
    width = w_all.shape[-1] // 2
    bn = COL_TILE
    bm, ni, kc = _tiles(m, d)
    nj = width // bn
    act_idx, out_idx, (wa_idx, wb_idx) = _stream_maps(ni, nj, lead, [0, nj])
    n_out = 1 if act == "swiglu" else 2
    wspec = lambda idx: pl.BlockSpec((None,) * len(lead) + (kc, bn), idx)
    outs = pl.pallas_call(
        functools.partial(_pair_kernel, act=act),
        grid=(nj + 1, ni),
        in_specs=[pl.BlockSpec((bm, d), act_idx), pl.BlockSpec((bm, LANES), act_idx),
                  wspec(wa_idx), wspec(wb_idx)],
        out_specs=[pl.BlockSpec((bm, bn), out_idx)] * n_out,
        out_shape=[jax.ShapeDtypeStruct((m, width), BF16)] * n_out,
        scratch_shapes=[pltpu.VMEM((2, 2, d, bn), BF16)],
        compiler_params=_cparams(2),
        name=name,
    )(hb, rstd, w_all, w_all)
    return outs[0] if n_out == 1 else outs


def _mixer_in_kernel(hb_ref, rstd_ref, wt_ref, proj_ref, wbuf_ref):
    def compute(w):
        hb = hb_ref[...]
        bn = proj_ref.shape[1]
        sub = min(SUB_COLS, bn)
        r = _lane_tile(rstd_ref[...], sub // LANES)
        for c in range(0, bn, sub):
            proj_ref[:, c:c + sub] = (_dot_nt(hb, w[0][c:c + sub, :]) * r).astype(proj_ref.dtype)

    _stream_step([wt_ref], wbuf_ref, compute)


def _mixer_in(hb, rstd, wt_all, lead, *, n_cols, dest):
    m, d = hb.shape
    bn = COL_TILE
    bm, ni, _ = _tiles(m, d)
    rc = bn // ni
    assert bn % ni == 0 and rc % 16 == 0
    nj = n_cols // bn
    act_idx, out_idx, _ = _stream_maps(ni, nj, lead, [0], dest)

    def wt_idx(jj, i):
        return lead + (jnp.minimum(jj, nj - 1) * ni + jnp.where(jj == nj, ni - 1, i), 0)

    return pl.pallas_call(
        _mixer_in_kernel,
        grid=(nj + 1, ni),
        in_specs=[pl.BlockSpec((bm, d), act_idx), pl.BlockSpec((bm, LANES), act_idx),
                  pl.BlockSpec((None,) * len(lead) + (rc, d), wt_idx)],
        out_specs=pl.BlockSpec((bm, bn), out_idx),
        out_shape=jax.ShapeDtypeStruct((m, n_cols), BF16),
        scratch_shapes=[pltpu.VMEM((2, 1, bn, d), BF16)],
        compiler_params=_cparams(2),
        name="mixer_in",
    )(hb, rstd, wt_all)


def _residual_kernel(*refs, n_a, alpha, with_norm, d_model):
    a_refs = refs[:n_a]
    w_ref, x_ref = refs[n_a], refs[n_a + 1]
    if with_norm:
        g_ref, xo_ref, hb_ref, rstd_ref, wbuf_ref, ssq_ref = refs[n_a + 2:]
    else:
        xo_ref, wbuf_ref = refs[n_a + 2:]
    jj = pl.program_id(0)
    i = pl.program_id(1)

    def compute(w):
        ka = a_refs[0].shape[1]
        bm, bn = xo_ref.shape
        sub = min(SUB_COLS, bn)
        a_vals = [a_ref[...] for a_ref in a_refs]
        part = None
        for c in range(0, bn, sub):
            acc = _dot(a_vals[0], w[0][0:ka, c:c + sub])
            for k in range(1, n_a):
                acc = acc + _dot(a_vals[k], w[0][k * ka:(k + 1) * ka, c:c + sub])
            xn = x_ref[:, c:c + sub] + (acc if alpha == 1.0 else alpha * acc)
            xo_ref[:, c:c + sub] = xn
            if with_norm:
                hb_ref[:, c:c + sub] = (xn * g_ref[:, c:c + sub]).astype(hb_ref.dtype)
                sq = _row_sumsq_lanes(xn)
                part = sq if part is None else part + sq
        if not with_norm:
            return
        rows = pl.ds(pl.multiple_of(i * bm, bm), bm)

        @pl.when(jj == 1)
        def _():
            ssq_ref[rows, :] = part

        @pl.when(jj > 1)
        def _():
            ssq_ref[rows, :] += part

        @pl.when(jj == pl.num_programs(0) - 1)
        def _():
            ssq = jnp.sum(ssq_ref[rows, :], axis=-1, keepdims=True)
            rstd_ref[...] = jnp.broadcast_to(lax.rsqrt(ssq / d_model + EPS), rstd_ref.shape)

    _stream_step([w_ref], wbuf_ref, compute)


def _residual_proj(a_list, w_all, lead, x, gain_next, *, alpha):
    m, d = x.shape
    n_a = len(a_list)
    ka = a_list[0].shape[1]
    k_total = ka * n_a
    assert all(a.shape[1] == ka for a in a_list) and k_total == w_all.shape[-2]
    bn = min(RES_COL_TILE, d)
    bm, ni, kc = _tiles(m, k_total, RES_ROW_TILE)
    nj = d // bn
    with_norm = gain_next is not None
    act_idx, out_idx, (w_idx,) = _stream_maps(ni, nj, lead, [0])
    in_specs = [pl.BlockSpec((bm, ka), act_idx) for _ in range(n_a)]
    in_specs += [pl.BlockSpec((None,) * len(lead) + (kc, bn), w_idx), pl.BlockSpec((bm, bn), out_idx)]
    operands = list(a_list) + [w_all, x]
    out_specs = [pl.BlockSpec((bm, bn), out_idx)]
    out_shape = [jax.ShapeDtypeStruct((m, d), F32)]
    scratch = [pltpu.VMEM((2, 1, k_total, bn), BF16)]
    if with_norm:
        in_specs += [pl.BlockSpec((1, bn), lambda jj, i: (0, jnp.maximum(jj - 1, 0)))]
        operands += [gain_next.reshape(1, d)]
        out_specs += [pl.BlockSpec((bm, bn), out_idx),
                      pl.BlockSpec((bm, LANES), lambda jj, i: (jnp.where(jj == nj, i, 0), 0))]
        out_shape += [jax.ShapeDtypeStruct((m, d), BF16),
                      jax.ShapeDtypeStruct((m, LANES), F32)]
        scratch += [pltpu.VMEM((m, LANES), F32)]
    outs = pl.pallas_call(
        functools.partial(_residual_kernel, n_a=n_a, alpha=alpha, with_norm=with_norm, d_model=d),
        grid=(nj + 1, ni),
        in_specs=in_specs,
        out_specs=out_specs,
        out_shape=out_shape,
        scratch_shapes=scratch,
        compiler_params=_cparams(2),
        name="residual_proj",
    )(*operands)
    if with_norm:
        return outs[0], outs[1], outs[2]
    return outs[0], None, None


def _dt_kernel(hb_ref, rstd_ref, wt_ref, dt_ref, *, n_heads):
    row = lax.broadcasted_iota(jnp.int32, wt_ref.shape, 0)
    wt = jnp.where(row < n_heads, wt_ref[...], 0.0).astype(BF16)
    dt_ref[...] = _dot_nt(hb_ref[...], wt) * rstd_ref[...]


def _dt_proj(hb, rstd, wt_all, lead, *, dt_col, n_heads):
    m, d = hb.shape
    bm = min(ROW_TILE, m)
    assert dt_col % LANES == 0
    return pl.pallas_call(
        functools.partial(_dt_kernel, n_heads=n_heads),
        grid=(m // bm,),
        in_specs=[pl.BlockSpec((bm, d), lambda i: (i, 0)),
                  pl.BlockSpec((bm, LANES), lambda i: (i, 0)),
                  pl.BlockSpec((None,) * len(lead) + (LANES, d), lambda i: lead + (dt_col // LANES, 0))],
        out_specs=pl.BlockSpec((bm, LANES), lambda i: (i, 0)),
        out_shape=jax.ShapeDtypeStruct((m, LANES), F32),
        compiler_params=_cparams(1),
        name="dt_proj",
    )(hb, rstd, wt_all)


def _head_ones():
    r = lax.broadcasted_iota(jnp.int32, (LANES, LANES), 0) // HEAD_DIM
    c = lax.broadcasted_iota(jnp.int32, (LANES, LANES), 1) // HEAD_DIM
    return (r == c).astype(BF16)


def _head_rms(v, gain_row):
    rows, width = v.shape
    nt = width // LANES
    ones = _head_ones()
    sq = v * v
    stacked = jnp.concatenate([sq[:, t * LANES:(t + 1) * LANES] for t in range(nt)], axis=0)
    hi, lo = _split2(stacked)
    ssq = _dot(hi, ones) + _dot(lo, ones)
    rs = lax.rsqrt(ssq / HEAD_DIM + EPS)
    return [v[:, t * LANES:(t + 1) * LANES] * rs[t * rows:(t + 1) * rows] * gain_row for t in range(nt)]


def _attn_kernel(q_ref, kp_ref, kc_ref, vp_ref, vc_ref, qg_ref, kg_ref, sink_ref, o_ref):
    n = pl.program_id(0)
    w = WINDOW
    n_kv = kc_ref.shape[1] // HEAD_DIM
    n_tiles = q_ref.shape[1] // LANES
    hd = HEAD_DIM

    qg = qg_ref[...]
    qn = []
    for t in range(n_tiles):
        qt = q_ref[:, t * LANES:(t + 1) * LANES].astype(F32).T
        for half in range(2):
            blk = qt[half * hd:(half + 1) * hd, :]
            rs = lax.rsqrt(jnp.sum(blk * blk, axis=0, keepdims=True) / hd + EPS)
            qn.append((blk * rs * qg).astype(BF16))

    kk = jnp.concatenate([kp_ref[...], kc_ref[...]], axis=0).astype(F32)
    k_tiles = [kt.astype(BF16) for kt in _head_rms(kk, kg_ref[...])]
    vv = jnp.concatenate([vp_ref[...], vc_ref[...]], axis=0).astype(F32)
    vt_tiles = [vv[:, a * LANES:(a + 1) * LANES].T.astype(BF16) for a in range(n_kv // 2)]

    kj = lax.broadcasted_iota(jnp.int32, (2 * w, w), 0)
    qi = lax.broadcasted_iota(jnp.int32, (2 * w, w), 1)
    first_key = jnp.where(n > 0, 0, w)
    valid = (kj > qi) & (kj <= qi + w) & (kj >= first_key)
    zeros = jnp.zeros((hd, w), BF16)

    for h in range(n_kv):
        a, half = h // 2, h % 2
        cols = []
        for g in range(KV_GROUP):
            qh = qn[h * KV_GROUP + g]
            cols.append(jnp.concatenate([qh, zeros] if half == 0 else [zeros, qh], axis=0))
        st = _dot(k_tiles[a], jnp.concatenate(cols, axis=1))
        p_cols, inv_cols = [], []
        for g in range(KV_GROUP):
            s = jnp.where(valid, st[:, g * w:(g + 1) * w], NEG_BIG)
            sk = sink_ref[h:h + 1, g * w:(g + 1) * w]
            mx = jnp.maximum(jnp.max(s, axis=0, keepdims=True), sk)
            p = jnp.exp(s - mx)
            den = jnp.sum(p, axis=0, keepdims=True) + jnp.exp(sk - mx)
            p_cols.append(p.astype(BF16))
            inv_cols.append(1.0 / den)
        vt = vt_tiles[a][half * hd:(half + 1) * hd, :]
        ot = _dot(vt, jnp.concatenate(p_cols, axis=1)) * jnp.concatenate(inv_cols, axis=1)
        for pr in range(KV_GROUP // 2):
            pair = jnp.concatenate([ot[:, (2 * pr) * w:(2 * pr + 1) * w],
                                    ot[:, (2 * pr + 1) * w:(2 * pr + 2) * w]], axis=0)
            t = h * (KV_GROUP // 2) + pr
            o_ref[:, t * LANES:(t + 1) * LANES] = pair.T.astype(o_ref.dtype)


def _attention(proj, q_gain, k_gain, sinks, *, q_width, k_col, v_col, kv_width):
    s_len = proj.shape[0]
    nb = s_len // WINDOW
    n_kv = kv_width // HEAD_DIM
    kb, vb = k_col // kv_width, v_col // kv_width
    qg = jnp.broadcast_to((q_gain.astype(F32) * HEAD_DIM ** -0.5)[:, None], (HEAD_DIM, WINDOW))
    kg = jnp.concatenate([k_gain, k_gain]).reshape(1, LANES).astype(F32)
    sink_rows = jnp.repeat(sinks.astype(F32), WINDOW).reshape(n_kv, KV_GROUP * WINDOW)
    prev = lambda n: jnp.maximum(n - 1, 0)
    const = lambda shape: pl.BlockSpec(shape, lambda n: (0, 0))
    return pl.pallas_call(
        _attn_kernel,
        grid=(nb,),
        in_specs=[pl.BlockSpec((WINDOW, q_width), lambda n: (n, 0)),
                  pl.BlockSpec((WINDOW, kv_width), lambda n: (prev(n), kb)),
                  pl.BlockSpec((WINDOW, kv_width), lambda n: (n, kb)),
                  pl.BlockSpec((WINDOW, kv_width), lambda n: (prev(n), vb)),
                  pl.BlockSpec((WINDOW, kv_width), lambda n: (n, vb)),
                  const((HEAD_DIM, WINDOW)), const((1, LANES)), const((n_kv, KV_GROUP * WINDOW))],
        out_specs=pl.BlockSpec((WINDOW, q_width), lambda n: (n, 0)),
        out_shape=jax.ShapeDtypeStruct((s_len, q_width), BF16),
        compiler_params=_cparams(1),
        name="swa_attention",
    )(proj, proj, proj, proj, proj, qg, kg, sink_rows)


def _ssd_kernel(z_ref, xs_ref, b_ref, c_ref, dt_ref, cwx_ref, cwb_ref, cwc_ref, cbx_ref, cbb_ref, cbc_ref,
                dtb_ref, a_ref, dsk_ref, ng_ref, o_ref,
                px_ref, pb_ref, pc_ref, state_ref):
    c_idx = pl.program_id(0)
    L = CHUNK
    hp = xs_ref.shape[1]
    n_heads = hp // HEAD_DIM
    gw = hp // SSM_GROUPS
    hist = 8

    @pl.when(c_idx == 0)
    def _():
        px_ref[0:hist, :] = jnp.zeros((hist, px_ref.shape[1]), F32)
        pb_ref[0:hist, :] = jnp.zeros((hist, pb_ref.shape[1]), F32)
        pc_ref[0:hist, :] = jnp.zeros((hist, pc_ref.shape[1]), F32)
        state_ref[...] = jnp.zeros(state_ref.shape, F32)

    def conv_silu(src_ref, pad_ref, w_ref, bias_ref):
        pad_ref[hist:hist + L, :] = src_ref[...].astype(F32)
        acc = bias_ref[...]
        for k in range(SSM_CONV):
            acc = acc + pad_ref[pl.ds(hist - (SSM_CONV - 1) + k, L), :] * w_ref[k:k + 1, :]
        pad_ref[0:hist, :] = pad_ref[L:L + hist, :]
        return _silu(acc)

    xs = conv_silu(xs_ref, px_ref, cwx_ref, cbx_ref)
    bm = conv_silu(b_ref, pb_ref, cwb_ref, cbb_ref)
    cm = conv_silu(c_ref, pc_ref, cwc_ref, cbc_ref)

    dt_in = dt_ref[...] + dtb_ref[...]
    dt = jnp.maximum(dt_in, 0.0) + jnp.log1p(jnp.exp(-jnp.abs(dt_in)))
    da = dt * a_ref[...]
    ri = lax.broadcasted_iota(jnp.int32, (L, L), 0)
    ci = lax.broadcasted_iota(jnp.int32, (L, L), 1)
    causal = ci <= ri
    tril = causal.astype(BF16)
    p0, p1, p2 = _split3(da)
    a_cs = _dot(tril, p0) + _dot(tril, p1) + _dot(tril, p2)
    a_cs_t = a_cs.T
    a_last = a_cs[L - 1:L, :]
    decay_end = jnp.exp(a_last - a_cs)
    e_acs = jnp.exp(a_cs)

    er = lax.broadcasted_iota(jnp.int32, (LANES, hp), 0)
    ec = lax.broadcasted_iota(jnp.int32, (LANES, hp), 1) // HEAD_DIM
    expand = (er == ec).astype(BF16)

    def expand_heads(v):
        hi, lo = _split2(v)
        return _dot(hi, expand) + _dot(lo, expand)

    dt_e = expand_heads(dt)
    xdt = xs * dt_e
    x_end = (xdt * expand_heads(decay_end)).astype(BF16)
    e_acs_e = expand_heads(e_acs)
    chunk_decay_e = e_acs_e[L - 1:L, :]
    xdt_b = xdt.astype(BF16)

    lane2 = lax.broadcasted_iota(jnp.int32, (L, LANES), 1)
    low_half = lane2 < HEAD_DIM
    y_tiles = []
    for g in range(SSM_GROUPS):
        bg = bm[:, g * SSM_STATE:(g + 1) * SSM_STATE]
        cg = cm[:, g * SSM_STATE:(g + 1) * SSM_STATE].astype(BF16)
        cb = _dot_nt(cg, bg.astype(BF16))
        heads_per_group = n_heads // SSM_GROUPS
        for pr in range(heads_per_group // 2):
            h0 = g * heads_per_group + 2 * pr
            t = h0 // 2
            ms = []
            for h in (h0, h0 + 1):
                seg = a_cs[:, h:h + 1] - a_cs_t[h:h + 1, :]
                ms.append((cb * jnp.exp(jnp.where(causal, seg, NEG_BIG))).astype(BF16))
            xt = xdt_b[:, t * LANES:(t + 1) * LANES]
            zero = jnp.zeros_like(xt)
            rhs = jnp.concatenate([jnp.where(low_half, xt, zero), jnp.where(low_half, zero, xt)], axis=0)
            y_tiles.append(_dot(jnp.concatenate(ms, axis=1), rhs))
    y_diag = jnp.concatenate(y_tiles, axis=1)

    y_off_tiles, new_state = [], []
    for g in range(SSM_GROUPS):
        sl = slice(g * gw, (g + 1) * gw)
        st = state_ref[:, sl]
        cg = cm[:, g * SSM_STATE:(g + 1) * SSM_STATE].astype(BF16)
        y_off_tiles.append(_dot(cg, st.astype(BF16)))
        bg_t = bm[:, g * SSM_STATE:(g + 1) * SSM_STATE].T.astype(BF16)
        new_state.append(st * chunk_decay_e[:, sl] + _dot(bg_t, x_end[:, sl]))
    y_off = jnp.concatenate(y_off_tiles, axis=1) * e_acs_e
    for g in range(SSM_GROUPS):
        state_ref[:, g * gw:(g + 1) * gw] = new_state[g]

    y = y_diag + y_off + xs * dsk_ref[...]
    gated = y * _silu(z_ref[...].astype(F32))
    outs = []
    for g in range(SSM_GROUPS):
        gg = gated[:, g * gw:(g + 1) * gw]
        ms_ = jnp.sum(gg * gg, axis=-1, keepdims=True) / gw
        outs.append(gg * lax.rsqrt(ms_ + EPS))
    o_ref[...] = (jnp.concatenate(outs, axis=1) * ng_ref[...]).astype(o_ref.dtype)


def _ssd(proj, dt_raw, conv_w, conv_b, dt_bias, a_log, d_skip, norm_g, *, z_col, xs_col, b_col, c_col, hp):
    s_len = proj.shape[0]
    nc = s_len // CHUNK
    n_heads = hp // HEAD_DIM
    bc = SSM_GROUPS * SSM_STATE
    row = lambda v: v.reshape(1, -1).astype(F32)
    pad_heads = lambda v: jnp.pad(v.astype(F32), (0, LANES - n_heads)).reshape(1, LANES)
    cw_x, cw_b, cw_c = conv_w[:, :hp], conv_w[:, hp:hp + bc], conv_w[:, hp + bc:]
    cb_x, cb_b, cb_c = row(conv_b[:hp]), row(conv_b[hp:hp + bc]), row(conv_b[hp + bc:])
    a_neg = pad_heads(-jnp.exp(a_log.astype(F32)))
    const = lambda shape: pl.BlockSpec(shape, lambda c: (0, 0))
    return pl.pallas_call(
        _ssd_kernel,
        grid=(nc,),
        in_specs=[pl.BlockSpec((CHUNK, hp), lambda c: (c, z_col // hp)),
                  pl.BlockSpec((CHUNK, hp), lambda c: (c, xs_col // hp)),
                  pl.BlockSpec((CHUNK, bc), lambda c: (c, b_col // bc)),
                  pl.BlockSpec((CHUNK, bc), lambda c: (c, c_col // bc)),
                  pl.BlockSpec((CHUNK, LANES), lambda c: (c, 0)),
                  const((SSM_CONV, hp)), const((SSM_CONV, bc)), const((SSM_CONV, bc)),
                  const((1, hp)), const((1, bc)), const((1, bc)),
                  const((1, LANES)), const((1, LANES)), const((1, hp)), const((1, hp))],
        out_specs=pl.BlockSpec((CHUNK, hp), lambda c: (c, 0)),
        out_shape=jax.ShapeDtypeStruct((s_len, hp), BF16),
        scratch_shapes=[pltpu.VMEM((CHUNK + 8, hp), F32),
                        pltpu.VMEM((CHUNK + 8, bc), F32),
                        pltpu.VMEM((CHUNK + 8, bc), F32),
                        pltpu.VMEM((SSM_STATE, hp), F32)],
        compiler_params=_cparams(1),
        name="ssd_mixer",
    )(proj, proj, proj, proj, dt_raw,
      cw_x.astype(F32), cw_b.astype(F32), cw_c.astype(F32), cb_x, cb_b, cb_c,
      pad_heads(dt_bias), a_neg, row(jnp.repeat(d_skip, HEAD_DIM)), row(norm_g))


def _gate_kernel(u_ref, v_ref, lg_ref, lb_ref, ws_ref, bst_ref, o_ref):
    L = CHUNK
    width = u_ref.shape[1]
    gw = width // GM_GROUPS
    groups = [slice(g * gw, (g + 1) * gw) for g in range(GM_GROUPS)]
    s1 = s2 = None
    for sl in groups:
        vf = v_ref[:, sl].astype(F32)
        s1 = vf if s1 is None else s1 + vf
        s2 = vf * vf if s2 is None else s2 + vf * vf
    mean = jnp.sum(s1, axis=-1, keepdims=True) / width
    var = jnp.sum(s2, axis=-1, keepdims=True) / width - mean * mean
    rs = lax.rsqrt(var + EPS)
    ri = lax.broadcasted_iota(jnp.int32, (L, L), 0)
    ci = lax.broadcasted_iota(jnp.int32, (L, L), 1)
    causal = ci <= ri
    reps = L // lg_ref.shape[0]
    for g, sl in enumerate(groups):
        vh = ((v_ref[:, sl].astype(F32) - mean) * rs).astype(BF16)
        vn = (vh * jnp.concatenate([lg_ref[:, sl]] * reps, axis=0)
              + jnp.concatenate([lb_ref[:, sl]] * reps, axis=0))
        wc = jnp.where(causal, ws_ref[g], 0.0).astype(BF16)
        sv = _dot(wc, vn) + bst_ref[:, g:g + 1]
        o_ref[:, sl] = u_ref[:, sl] * sv.astype(BF16)


def _gmlp_gate(u, v, ln_g, ln_b, ws, bs):
    s_len, width = u.shape
    nc = s_len // CHUNK
    row = lambda t: jnp.broadcast_to(t.astype(BF16)[None, :], (BF16_SUBLANES, width))
    return pl.pallas_call(
        _gate_kernel,
        grid=(nc,),
        in_specs=[pl.BlockSpec((CHUNK, width), lambda c: (c, 0)),
                  pl.BlockSpec((CHUNK, width), lambda c: (c, 0)),
                  pl.BlockSpec((BF16_SUBLANES, width), lambda c: (0, 0)),
                  pl.BlockSpec((BF16_SUBLANES, width), lambda c: (0, 0)),
                  pl.BlockSpec((GM_GROUPS, CHUNK, CHUNK), lambda c: (0, 0, 0)),
                  pl.BlockSpec((CHUNK, GM_GROUPS), lambda c: (0, 0))],
        out_specs=pl.BlockSpec((CHUNK, width), lambda c: (c, 0)),
        out_shape=jax.ShapeDtypeStruct((s_len, width), BF16),
        compiler_params=_cparams(1),
        name="gmlp_gate",
    )(u, v, row(ln_g), row(ln_b), ws.astype(F32), bs.T.astype(F32))


def kernel(x, ffn_norm, ffn_w13, ffn_w2, mix_norm, hy_w_in, attn_q_norm, attn_k_norm, attn_sinks,
           ssm_conv_w, ssm_conv_b, ssm_dt_bias, ssm_A_log, ssm_D, ssm_norm, hy_w_out,
           gm_w_in, gm_ln_g, gm_ln_b, gm_ws, gm_bs, gm_w_out):
    bsz, s_len, d = x.shape
    depth = ffn_norm.shape[0]
    n_attn_heads = attn_sinks.shape[1]
    n_ssm_heads = ssm_D.shape[1]
    q_width = n_attn_heads * HEAD_DIM
    kv_width = (n_attn_heads // KV_GROUP) * HEAD_DIM
    hp = n_ssm_heads * HEAD_DIM
    bc = SSM_GROUPS * SSM_STATE
    assert bsz == 1 and s_len % CHUNK == 0

    bn = COL_TILE
    nq, nkv, nzx = q_width // bn, 2 * kv_width // bn, 2 * hp // bn
    assert q_width % bn == 0 and (2 * kv_width) % bn == 0 and hp % bn == 0 and bc % bn == 0

    def proj_dest(j):
        return j + jnp.where(j < nq, 0, jnp.where(j < nq + nkv, nzx, jnp.where(j < nq + nkv + nzx, -nkv, 0)))

    n_proj = q_width + 2 * kv_width + 2 * hp + 2 * bc
    p_z, p_x = q_width, q_width + hp
    p_k = p_x + hp
    p_v = p_k + kv_width
    p_b = p_v + kv_width
    p_c = p_b + bc

    w_in_t = jnp.swapaxes(hy_w_in, 1, 2)

    xm = x.reshape(s_len, d)
    hb, rstd = _prep(xm, ffn_norm[0, 0])

    for l in range(depth):
        i = l // 2
        g_act = _pair_proj(hb, rstd, ffn_w13, (l, 0), act="swiglu", name="ffn_up")
        xm, hb, rstd = _residual_proj([g_act], ffn_w2, (l, 0), xm, mix_norm[l], alpha=0.5)
        if l % 2 == 0:
            proj = _mixer_in(hb, rstd, w_in_t, (i,), n_cols=n_proj, dest=proj_dest)
            dt_raw = _dt_proj(hb, rstd, w_in_t, (i,), dt_col=n_proj, n_heads=n_ssm_heads)
            attn = _attention(proj, attn_q_norm[i], attn_k_norm[i], attn_sinks[i],
                              q_width=q_width, k_col=p_k, v_col=p_v, kv_width=kv_width)
            ssm = _ssd(proj, dt_raw, ssm_conv_w[i], ssm_conv_b[i], ssm_dt_bias[i], ssm_A_log[i],
                       ssm_D[i], ssm_norm[i], z_col=p_z, xs_col=p_x, b_col=p_b, c_col=p_c, hp=hp)
            xm, hb, rstd = _residual_proj([attn, ssm], hy_w_out, (i,), xm, ffn_norm[l, 1], alpha=1.0)
        else:
            u, v = _pair_proj(hb, rstd, gm_w_in, (i,), act="gelu", name="gmlp_in")
            a = _gmlp_gate(u, v, gm_ln_g[i], gm_ln_b[i], gm_ws[i], gm_bs[i])
            xm, hb, rstd = _residual_proj([a], gm_w_out, (i,), xm, ffn_norm[l, 1], alpha=1.0)
        g_act = _pair_proj(hb, rstd, ffn_w13, (l, 1), act="swiglu", name="ffn_up")
        nxt = ffn_norm[l + 1, 0] if l + 1 < depth else None
        xm, hb, rstd = _residual_proj([g_act], ffn_w2, (l, 1), xm, nxt, alpha=0.5)
    return xm.reshape(bsz, s_len, d)
```

```python
import functools

import jax
import jax.numpy as jnp
from jax import lax
from jax.experimental import pallas as pl
from jax.experimental.pallas import tpu as pltpu

F32 = jnp.float32
BF16 = jnp.bfloat16

EPS = 1e-6
LANES = 128
BF16_SUBLANES = 16
HEAD_DIM = 64
WINDOW = 128
KV_GROUP = 8
SSM_GROUPS = 4
SSM_STATE = 128
SSM_CONV = 4
CHUNK = 128
GM_GROUPS = 8
NEG_BIG = -1e30
VMEM_LIMIT = 62 * 1024 * 1024
ROW_TILE = 1024
COL_TILE = 512
RES_ROW_TILE = 512
RES_COL_TILE = 1024
SUB_COLS = 256


def _cparams(n_axes):
    return pltpu.CompilerParams(dimension_semantics=("arbitrary",) * n_axes,
                                vmem_limit_bytes=VMEM_LIMIT)


def _lane_tile(v, reps):
    return v if reps == 1 else jnp.concatenate([v] * reps, axis=1)


def _dot(a, b):
    return jnp.dot(a, b, preferred_element_type=F32)


def _dot_nt(a, b):
    return lax.dot_general(a, b, (((1,), (1,)), ((), ())), preferred_element_type=F32)


def _split2(v):
    hi = v.astype(BF16)
    lo = (v - hi.astype(F32)).astype(BF16)
    return hi, lo


def _split3(v):
    hi = v.astype(BF16)
    r = v - hi.astype(F32)
    mid = r.astype(BF16)
    lo = (r - mid.astype(F32)).astype(BF16)
    return hi, mid, lo


def _sigmoid(v):
    return 1.0 / (1.0 + jnp.exp(-v))


def _silu(v):
    return v * _sigmoid(v)


def _gelu_tanh(v):
    c = 0.7978845608028654
    return 0.5 * v * (1.0 + jnp.tanh(c * (v + 0.044715 * (v * v * v))))


def _row_sumsq_lanes(v):
    n = v.shape[1] // LANES
    acc = v[:, :LANES] * v[:, :LANES]
    for t in range(1, n):
        s = v[:, t * LANES:(t + 1) * LANES]
        acc = acc + s * s
    return acc


def _prep_kernel(x_ref, g_ref, hb_ref, rstd_ref, *, d_model):
    x = x_ref[...]
    hb_ref[...] = (x * g_ref[...]).astype(BF16)
    ssq = jnp.sum(x * x, axis=-1, keepdims=True)
    rstd_ref[...] = jnp.broadcast_to(lax.rsqrt(ssq / d_model + EPS), rstd_ref.shape)


def _prep(x, gain):
    m, d = x.shape
    bm = min(256, m)
    return pl.pallas_call(
        functools.partial(_prep_kernel, d_model=d),
        grid=(m // bm,),
        in_specs=[pl.BlockSpec((bm, d), lambda i: (i, 0)),
                  pl.BlockSpec((1, d), lambda i: (0, 0))],
        out_specs=[pl.BlockSpec((bm, d), lambda i: (i, 0)),
                   pl.BlockSpec((bm, LANES), lambda i: (i, 0))],
        out_shape=[jax.ShapeDtypeStruct((m, d), BF16),
                   jax.ShapeDtypeStruct((m, LANES), F32)],
        compiler_params=_cparams(1),
        name="prep_norm",
    )(x, gain.reshape(1, d))


def _stream_step(w_chunk_refs, wbuf_ref, compute):
    jj = pl.program_id(0)
    i = pl.program_id(1)
    kc = w_chunk_refs[0].shape[0]

    def cast(slot):
        row0 = pl.multiple_of(i * kc, kc)
        for w, ref in enumerate(w_chunk_refs):
            wbuf_ref[slot, w, pl.ds(row0, kc), :] = ref[...].astype(BF16)

    @pl.when(jj == 0)
    def _():
        cast(0)

    for parity in (0, 1):
        @pl.when((jj > 0) & (jj % 2 == parity))
        def _():
            compute([wbuf_ref.at[1 - parity, w] for w in range(len(w_chunk_refs))])
            cast(parity)


def _stream_maps(ni, nj, lead, col_offsets, dest=None):
    def act_idx(jj, i):
        return (jnp.where(jj == 0, 0, i), 0)

    def out_idx(jj, i):
        j = jnp.maximum(jj - 1, 0)
        return (jnp.where(jj == 0, 0, i), j if dest is None else dest(j))

    def w_idx(off):
        return lambda jj, i: lead + (jnp.where(jj == nj, ni - 1, i), jnp.minimum(jj, nj - 1) + off)

    return act_idx, out_idx, [w_idx(off) for off in col_offsets]


def _tiles(m, k, row_tile=None):
    bm = min(row_tile or ROW_TILE, m)
    ni = m // bm
    assert m % bm == 0 and k % ni == 0 and (k // ni) % 16 == 0
    return bm, ni, k // ni


def _pair_kernel(hb_ref, rstd_ref, wa_ref, wb_ref, *rest, act):
    out_refs, wbuf_ref = rest[:-1], rest[-1]

    def compute(w):
        hb = hb_ref[...]
        bn = out_refs[0].shape[1]
        sub = min(SUB_COLS, bn)
        r = _lane_tile(rstd_ref[...], sub // LANES)
        for c in range(0, bn, sub):
            a = _dot(hb, w[0][:, c:c + sub]) * r
            b = _dot(hb, w[1][:, c:c + sub]) * r
            if act == "swiglu":
                out_refs[0][:, c:c + sub] = (_silu(a) * b).astype(out_refs[0].dtype)
            else:
                out_refs[0][:, c:c + sub] = _gelu_tanh(a).astype(out_refs[0].dtype)
                out_refs[1][:, c:c + sub] = _gelu_tanh(b).astype(out_refs[1].dtype)

    _stream_step([wa_ref, wb_ref], wbuf_ref, compute)


def _pair_proj(hb, rstd, w_all, lead, *, act, name):
    m, d = hb.shape
    width = w_all.shape[-1] // 2
    bn = COL_TILE
    bm, ni, kc = _tiles(m, d)
    nj = width // bn
    act_idx, out_idx, (wa_idx, wb_idx) = _stream_maps(ni, nj, lead, [0, nj])
    n_out = 1 if act == "swiglu" else 2
    wspec = lambda idx: pl.BlockSpec((None,) * len(lead) + (kc, bn), idx)
    outs = pl.pallas_call(
        functools.partial(_pair_kernel, act=act),
        grid=(nj + 1, ni),
        in_specs=[pl.BlockSpec((bm, d), act_idx), pl.BlockSpec((bm, LANES), act_idx),
                  wspec(wa_idx), wspec(wb_idx)],
        out_specs=[pl.BlockSpec((bm, bn), out_idx)] * n_out,
        out_shape=[jax.ShapeDtypeStruct((m, width), BF16)] * n_out,
        scratch_shapes=[pltpu.VMEM((2, 2, d, bn), BF16)],
        compiler_params=_cparams(2),
        name=name,
    )(hb, rstd, w_all, w_all)
    return outs[0] if n_out == 1 else outs


def _mixer_in_kernel(hb_ref, rstd_ref, wt_ref, proj_ref, wbuf_ref):
    def compute(w):
        hb = hb_ref[...]
        bn = proj_ref.shape[1]
        sub = min(SUB_COLS, bn)
        r = _lane_tile(rstd_ref[...], sub // LANES)
        for c in range(0, bn, sub):
            proj_ref[:, c:c + sub] = (_dot_nt(hb, w[0][c:c + sub, :]) * r).astype(proj_ref.dtype)

    _stream_step([wt_ref], wbuf_ref, compute)


def _mixer_in(hb, rstd, wt_all, lead, *, n_cols, dest):
    m, d = hb.shape
    bn = COL_TILE
    bm, ni, _ = _tiles(m, d)
    rc = bn // ni
    assert bn % ni == 0 and rc % 16 == 0
    nj = n_cols // bn
    act_idx, out_idx, _ = _stream_maps(ni, nj, lead, [0], dest)

    def wt_idx(jj, i):
        return lead + (jnp.minimum(jj, nj - 1) * ni + jnp.where(jj == nj, ni - 1, i), 0)

    return pl.pallas_call(
        _mixer_in_kernel,
        grid=(nj + 1, ni),
        in_specs=[pl.BlockSpec((bm, d), act_idx), pl.BlockSpec((bm, LANES), act_idx),
                  pl.BlockSpec((None,) * len(lead) + (rc, d), wt_idx)],
        out_specs=pl.BlockSpec((bm, bn), out_idx),
        out_shape=jax.ShapeDtypeStruct((m, n_cols), BF16),
        scratch_shapes=[pltpu.VMEM((2, 1, bn, d), BF16)],
        compiler_params=_cparams(2),
        name="mixer_in",
    )(hb, rstd, wt_all)


def _residual_kernel(*refs, n_a, alpha, with_norm, d_model):
    a_refs = refs[:n_a]
    w_ref, x_ref = refs[n_a], refs[n_a + 1]
    if with_norm:
        g_ref, xo_ref, hb_ref, rstd_ref, wbuf_ref, ssq_ref = refs[n_a + 2:]
    else:
        xo_ref, wbuf_ref = refs[n_a + 2:]
    jj = pl.program_id(0)
    i = pl.program_id(1)

    def compute(w):
        ka = a_refs[0].shape[1]
        bm, bn = xo_ref.shape
        sub = min(SUB_COLS, bn)
        a_vals = [a_ref[...] for a_ref in a_refs]
        part = None
        for c in range(0, bn, sub):
            acc = _dot(a_vals[0], w[0][0:ka, c:c + sub])
            for k in range(1, n_a):
                acc = acc + _dot(a_vals[k], w[0][k * ka:(k + 1) * ka, c:c + sub])
            xn = x_ref[:, c:c + sub] + (acc if alpha == 1.0 else alpha * acc)
            xo_ref[:, c:c + sub] = xn
            if with_norm:
                hb_ref[:, c:c + sub] = (xn * g_ref[:, c:c + sub]).astype(hb_ref.dtype)
                sq = _row_sumsq_lanes(xn)
                part = sq if part is None else part + sq
        if not with_norm:
            return
        rows = pl.ds(pl.multiple_of(i * bm, bm), bm)

        @pl.when(jj == 1)
        def _():
            ssq_ref[rows, :] = part

        @pl.when(jj > 1)
        def _():
            ssq_ref[rows, :] += part

        @pl.when(jj == pl.num_programs(0) - 1)
        def _():
            ssq = jnp.sum(ssq_ref[rows, :], axis=-1, keepdims=True)
            rstd_ref[...] = jnp.broadcast_to(lax.rsqrt(ssq / d_model + EPS), rstd_ref.shape)

    _stream_step([w_ref], wbuf_ref, compute)


def _residual_proj(a_list, w_all, lead, x, gain_next, *, alpha):
    m, d = x.shape
    n_a = len(a_list)
    ka = a_list[0].shape[1]
    k_total = ka * n_a
    assert all(a.shape[1] == ka for a in a_list) and k_total == w_all.shape[-2]
    bn = min(RES_COL_TILE, d)
    bm, ni, kc = _tiles(m, k_total, RES_ROW_TILE)
    nj = d // bn
    with_norm = gain_next is not None
    act_idx, out_idx, (w_idx,) = _stream_maps(ni, nj, lead, [0])
    in_specs = [pl.BlockSpec((bm, ka), act_idx) for _ in range(n_a)]
    in_specs += [pl.BlockSpec((None,) * len(lead) + (kc, bn), w_idx), pl.BlockSpec((bm, bn), out_idx)]
    operands = list(a_list) + [w_all, x]
    out_specs = [pl.BlockSpec((bm, bn), out_idx)]
    out_shape = [jax.ShapeDtypeStruct((m, d), F32)]
    scratch = [pltpu.VMEM((2, 1, k_total, bn), BF16)]
    if with_norm:
        in_specs += [pl.BlockSpec((1, bn), lambda jj, i: (0, jnp.maximum(jj - 1, 0)))]
        operands += [gain_next.reshape(1, d)]
        out_specs += [pl.BlockSpec((bm, bn), out_idx),
                      pl.BlockSpec((bm, LANES), lambda jj, i: (jnp.where(jj == nj, i, 0), 0))]
        out_shape += [jax.ShapeDtypeStruct((m, d), BF16),
                      jax.ShapeDtypeStruct((m, LANES), F32)]
        scratch += [pltpu.VMEM((m, LANES), F32)]
    outs = pl.pallas_call(
        functools.partial(_residual_kernel, n_a=n_a, alpha=alpha, with_norm=with_norm, d_model=d),
        grid=(nj + 1, ni),
        in_specs=in_specs,
        out_specs=out_specs,
        out_shape=out_shape,
        scratch_shapes=scratch,
        compiler_params=_cparams(2),
        name="residual_proj",
    )(*operands)
    if with_norm:
        return outs[0], outs[1], outs[2]
    return outs[0], None, None


def _dt_kernel(hb_ref, rstd_ref, wt_ref, dt_ref, *, n_heads):
    row = lax.broadcasted_iota(jnp.int32, wt_ref.shape, 0)
    wt = jnp.where(row < n_heads, wt_ref[...], 0.0).astype(BF16)
    dt_ref[...] = _dot_nt(hb_ref[...], wt) * rstd_ref[...]


def _dt_proj(hb, rstd, wt_all, lead, *, dt_col, n_heads):
    m, d = hb.shape
    bm = min(ROW_TILE, m)
    assert dt_col % LANES == 0
    return pl.pallas_call(
        functools.partial(_dt_kernel, n_heads=n_heads),
        grid=(m // bm,),
        in_specs=[pl.BlockSpec((bm, d), lambda i: (i, 0)),
                  pl.BlockSpec((bm, LANES), lambda i: (i, 0)),
                  pl.BlockSpec((None,) * len(lead) + (LANES, d), lambda i: lead + (dt_col // LANES, 0))],
        out_specs=pl.BlockSpec((bm, LANES), lambda i: (i, 0)),
        out_shape=jax.ShapeDtypeStruct((m, LANES), F32),
        compiler_params=_cparams(1),
        name="dt_proj",
    )(hb, rstd, wt_all)


def _head_ones():
    r = lax.broadcasted_iota(jnp.int32, (LANES, LANES), 0) // HEAD_DIM
    c = lax.broadcasted_iota(jnp.int32, (LANES, LANES), 1) // HEAD_DIM
    return (r == c).astype(BF16)


def _head_rms(v, gain_row):
    rows, width = v.shape
    nt = width // LANES
    ones = _head_ones()
    sq = v * v
    stacked = jnp.concatenate([sq[:, t * LANES:(t + 1) * LANES] for t in range(nt)], axis=0)
    hi, lo = _split2(stacked)
    ssq = _dot(hi, ones) + _dot(lo, ones)
    rs = lax.rsqrt(ssq / HEAD_DIM + EPS)
    return [v[:, t * LANES:(t + 1) * LANES] * rs[t * rows:(t + 1) * rows] * gain_row for t in range(nt)]


def _attn_kernel(q_ref, kp_ref, kc_ref, vp_ref, vc_ref, qg_ref, kg_ref, sink_ref, o_ref):
    n = pl.program_id(0)
    w = WINDOW
    n_kv = kc_ref.shape[1] // HEAD_DIM
    n_tiles = q_ref.shape[1] // LANES
    hd = HEAD_DIM

    qg = qg_ref[...]
    qn = []
    for t in range(n_tiles):
        qt = q_ref[:, t * LANES:(t + 1) * LANES].astype(F32).T
        for half in range(2):
            blk = qt[half * hd:(half + 1) * hd, :]
            rs = lax.rsqrt(jnp.sum(blk * blk, axis=0, keepdims=True) / hd + EPS)
            qn.append((blk * rs * qg).astype(BF16))

    kk = jnp.concatenate([kp_ref[...], kc_ref[...]], axis=0).astype(F32)
    k_tiles = [kt.astype(BF16) for kt in _head_rms(kk, kg_ref[...])]
    vv = jnp.concatenate([vp_ref[...], vc_ref[...]], axis=0).astype(F32)
    vt_tiles = [vv[:, a * LANES:(a + 1) * LANES].T.astype(BF16) for a in range(n_kv // 2)]

    kj = lax.broadcasted_iota(jnp.int32, (2 * w, w), 0)
    qi = lax.broadcasted_iota(jnp.int32, (2 * w, w), 1)
    first_key = jnp.where(n > 0, 0, w)
    valid = (kj > qi) & (kj <= qi + w) & (kj >= first_key)
    zeros = jnp.zeros((hd, w), BF16)

    for h in range(n_kv):
        a, half = h // 2, h % 2
        cols = []
        for g in range(KV_GROUP):
            qh = qn[h * KV_GROUP + g]
            cols.append(jnp.concatenate([qh, zeros] if half == 0 else [zeros, qh], axis=0))
        st = _dot(k_tiles[a], jnp.concatenate(cols, axis=1))
        p_cols, inv_cols = [], []
        for g in range(KV_GROUP):
            s = jnp.where(valid, st[:, g * w:(g + 1) * w], NEG_BIG)
            sk = sink_ref[h:h + 1, g * w:(g + 1) * w]
            mx = jnp.maximum(jnp.max(s, axis=0, keepdims=True), sk)
            p = jnp.exp(s - mx)
            den = jnp.sum(p, axis=0, keepdims=True) + jnp.exp(sk - mx)
            p_cols.append(p.astype(BF16))
            inv_cols.append(1.0 / den)
        vt = vt_tiles[a][half * hd:(half + 1) * hd, :]
        ot = _dot(vt, jnp.concatenate(p_cols, axis=1)) * jnp.concatenate(inv_cols, axis=1)
        for pr in range(KV_GROUP // 2):
            pair = jnp.concatenate([ot[:, (2 * pr) * w:(2 * pr + 1) * w],
                                    ot[:, (2 * pr + 1) * w:(2 * pr + 2) * w]], axis=0)
            t = h * (KV_GROUP // 2) + pr
            o_ref[:, t * LANES:(t + 1) * LANES] = pair.T.astype(o_ref.dtype)


def _attention(proj, q_gain, k_gain, sinks, *, q_width, k_col, v_col, kv_width):
    s_len = proj.shape[0]
    nb = s_len // WINDOW
    n_kv = kv_width // HEAD_DIM
    kb, vb = k_col // kv_width, v_col // kv_width
    qg = jnp.broadcast_to((q_gain.astype(F32) * HEAD_DIM ** -0.5)[:, None], (HEAD_DIM, WINDOW))
    kg = jnp.concatenate([k_gain, k_gain]).reshape(1, LANES).astype(F32)
    sink_rows = jnp.repeat(sinks.astype(F32), WINDOW).reshape(n_kv, KV_GROUP * WINDOW)
    prev = lambda n: jnp.maximum(n - 1, 0)
    const = lambda shape: pl.BlockSpec(shape, lambda n: (0, 0))
    return pl.pallas_call(
        _attn_kernel,
        grid=(nb,),
        in_specs=[pl.BlockSpec((WINDOW, q_width), lambda n: (n, 0)),
                  pl.BlockSpec((WINDOW, kv_width), lambda n: (prev(n), kb)),
                  pl.BlockSpec((WINDOW, kv_width), lambda n: (n, kb)),
                  pl.BlockSpec((WINDOW, kv_width), lambda n: (prev(n), vb)),
                  pl.BlockSpec((WINDOW, kv_width), lambda n: (n, vb)),
                  const((HEAD_DIM, WINDOW)), const((1, LANES)), const((n_kv, KV_GROUP * WINDOW))],
        out_specs=pl.BlockSpec((WINDOW, q_width), lambda n: (n, 0)),
        out_shape=jax.ShapeDtypeStruct((s_len, q_width), BF16),
        compiler_params=_cparams(1),
        name="swa_attention",
    )(proj, proj, proj, proj, proj, qg, kg, sink_rows)


def _ssd_kernel(z_ref, xs_ref, b_ref, c_ref, xsp_ref, bp_ref, cp_ref, dt_ref,
                cwx_ref, cwb_ref, cwc_ref, cbx_ref, cbb_ref, cbc_ref,
                dtb_ref, a_ref, dsk_ref, ng_ref, o_ref, state_ref):
    c_idx = pl.program_id(0)
    L = CHUNK
    hp = xs_ref.shape[1]
    n_heads = hp // HEAD_DIM
    gw = hp // SSM_GROUPS

    @pl.when(c_idx == 0)
    def _():
        state_ref[...] = jnp.zeros(state_ref.shape, F32)

    sr = lax.broadcasted_iota(jnp.int32, (L, 2 * L), 0)
    sc = lax.broadcasted_iota(jnp.int32, (L, 2 * L), 1)
    first_col = jnp.where(c_idx > 0, 0, L)
    shift = jnp.concatenate(
        [jnp.where((sc == sr + (L - (SSM_CONV - 1) + k)) & (sc >= first_col), 1.0, 0.0).astype(BF16)
         for k in range(SSM_CONV - 1)], axis=0)

    def conv_silu(src_ref, prev_ref, w_ref, bias_ref):
        cur = src_ref[...]
        taps = _dot(shift, jnp.concatenate([prev_ref[...], cur], axis=0))
        acc = bias_ref[...] + cur.astype(F32) * w_ref[SSM_CONV - 1:SSM_CONV, :]
        for k in range(SSM_CONV - 1):
            acc = acc + taps[k * L:(k + 1) * L, :] * w_ref[k:k + 1, :]
        return _silu(acc)

    xs = conv_silu(xs_ref, xsp_ref, cwx_ref, cbx_ref)
    bm = conv_silu(b_ref, bp_ref, cwb_ref, cbb_ref)
    cm = conv_silu(c_ref, cp_ref, cwc_ref, cbc_ref)

    dt_in = dt_ref[...] + dtb_ref[...]
    dt = jnp.maximum(dt_in, 0.0) + jnp.log1p(jnp.exp(-jnp.abs(dt_in)))
    da = dt * a_ref[...]
    ri = lax.broadcasted_iota(jnp.int32, (L, L), 0)
    ci = lax.broadcasted_iota(jnp.int32, (L, L), 1)
    causal = ci <= ri
    tril = causal.astype(BF16)
    p0, p1, p2 = _split3(da)
    a_cs = _dot(tril, p0) + _dot(tril, p1) + _dot(tril, p2)
    a_cs_t = a_cs.T
    a_last = a_cs[L - 1:L, :]
    decay_end = jnp.exp(a_last - a_cs)
    e_acs = jnp.exp(a_cs)

    er = lax.broadcasted_iota(jnp.int32, (LANES, hp), 0)
    ec = lax.broadcasted_iota(jnp.int32, (LANES, hp), 1) // HEAD_DIM
    expand = (er == ec).astype(BF16)

    def expand_heads(v):
        hi, lo = _split2(v)
        return _dot(hi, expand) + _dot(lo, expand)

    dt_e = expand_heads(dt)
    xdt = xs * dt_e
    x_end = (xdt * expand_heads(decay_end)).astype(BF16)
    e_acs_e = expand_heads(e_acs)
    chunk_decay_e = e_acs_e[L - 1:L, :]
    xdt_b = xdt.astype(BF16)

    lane2 = lax.broadcasted_iota(jnp.int32, (L, LANES), 1)
    low_half = lane2 < HEAD_DIM
    y_tiles = []
    for g in range(SSM_GROUPS):
        bg = bm[:, g * SSM_STATE:(g + 1) * SSM_STATE]
        cg = cm[:, g * SSM_STATE:(g + 1) * SSM_STATE].astype(BF16)
        cb = _dot_nt(cg, bg.astype(BF16))
        heads_per_group = n_heads // SSM_GROUPS
        for pr in range(heads_per_group // 2):
            h0 = g * heads_per_group + 2 * pr
            t = h0 // 2
            ms = []
            for h in (h0, h0 + 1):
                seg = a_cs[:, h:h + 1] - a_cs_t[h:h + 1, :]
                ms.append((cb * jnp.exp(jnp.where(causal, seg, NEG_BIG))).astype(BF16))
            xt = xdt_b[:, t * LANES:(t + 1) * LANES]
            zero = jnp.zeros_like(xt)
            rhs = jnp.concatenate([jnp.where(low_half, xt, zero), jnp.where(low_half, zero, xt)], axis=0)
            y_tiles.append(_dot(jnp.concatenate(ms, axis=1), rhs))
    y_diag = jnp.concatenate(y_tiles, axis=1)

    y_off_tiles, new_state = [], []
    for g in range(SSM_GROUPS):
        sl = slice(g * gw, (g + 1) * gw)
        st = state_ref[:, sl]
        cg = cm[:, g * SSM_STATE:(g + 1) * SSM_STATE].astype(BF16)
        y_off_tiles.append(_dot(cg, st.astype(BF16)))
        bg_t = bm[:, g * SSM_STATE:(g + 1) * SSM_STATE].T.astype(BF16)
        new_state.append(st * chunk_decay_e[:, sl] + _dot(bg_t, x_end[:, sl]))
    y_off = jnp.concatenate(y_off_tiles, axis=1) * e_acs_e
    for g in range(SSM_GROUPS):
        state_ref[:, g * gw:(g + 1) * gw] = new_state[g]

    y = y_diag + y_off + xs * dsk_ref[...]
    gated = y * _silu(z_ref[...].astype(F32))
    outs = []
    for g in range(SSM_GROUPS):
        gg = gated[:, g * gw:(g + 1) * gw]
        ms_ = jnp.sum(gg * gg, axis=-1, keepdims=True) / gw
        outs.append(gg * lax.rsqrt(ms_ + EPS))
    o_ref[...] = (jnp.concatenate(outs, axis=1) * ng_ref[...]).astype(o_ref.dtype)


def _ssd(proj, dt_raw, conv_w, conv_b, dt_bias, a_log, d_skip, norm_g, *, z_col, xs_col, b_col, c_col, hp):
    s_len = proj.shape[0]
    nc = s_len // CHUNK
    n_heads = hp // HEAD_DIM
    bc = SSM_GROUPS * SSM_STATE
    row = lambda v: v.reshape(1, -1).astype(F32)
    pad_heads = lambda v: jnp.pad(v.astype(F32), (0, LANES - n_heads)).reshape(1, LANES)
    cw_x, cw_b, cw_c = conv_w[:, :hp], conv_w[:, hp:hp + bc], conv_w[:, hp + bc:]
    cb_x, cb_b, cb_c = row(conv_b[:hp]), row(conv_b[hp:hp + bc]), row(conv_b[hp + bc:])
    a_neg = pad_heads(-jnp.exp(a_log.astype(F32)))
    const = lambda shape: pl.BlockSpec(shape, lambda c: (0, 0))
    prev = lambda c: jnp.maximum(c - 1, 0)
    return pl.pallas_call(
        _ssd_kernel,
        grid=(nc,),
        in_specs=[pl.BlockSpec((CHUNK, hp), lambda c: (c, z_col // hp)),
                  pl.BlockSpec((CHUNK, hp), lambda c: (c, xs_col // hp)),
                  pl.BlockSpec((CHUNK, bc), lambda c: (c, b_col // bc)),
                  pl.BlockSpec((CHUNK, bc), lambda c: (c, c_col // bc)),
                  pl.BlockSpec((CHUNK, hp), lambda c: (prev(c), xs_col // hp)),
                  pl.BlockSpec((CHUNK, bc), lambda c: (prev(c), b_col // bc)),
                  pl.BlockSpec((CHUNK, bc), lambda c: (prev(c), c_col // bc)),
                  pl.BlockSpec((CHUNK, LANES), lambda c: (c, 0)),
                  const((SSM_CONV, hp)), const((SSM_CONV, bc)), const((SSM_CONV, bc)),
                  const((1, hp)), const((1, bc)), const((1, bc)),
                  const((1, LANES)), const((1, LANES)), const((1, hp)), const((1, hp))],
        out_specs=pl.BlockSpec((CHUNK, hp), lambda c: (c, 0)),
        out_shape=jax.ShapeDtypeStruct((s_len, hp), BF16),
        scratch_shapes=[pltpu.VMEM((SSM_STATE, hp), F32)],
        compiler_params=_cparams(1),
        name="ssd_mixer",
    )(proj, proj, proj, proj, proj, proj, proj, dt_raw,
      cw_x.astype(F32), cw_b.astype(F32), cw_c.astype(F32), cb_x, cb_b, cb_c,
      pad_heads(dt_bias), a_neg, row(jnp.repeat(d_skip, HEAD_DIM)), row(norm_g))


def _gate_kernel(u_ref, v_ref, lg_ref, lb_ref, ws_ref, bst_ref, o_ref):
    L = CHUNK
    width = u_ref.shape[1]
    gw = width // GM_GROUPS
    groups = [slice(g * gw, (g + 1) * gw) for g in range(GM_GROUPS)]
    s1 = s2 = None
    for sl in groups:
        vf = v_ref[:, sl].astype(F32)
        s1 = vf if s1 is None else s1 + vf
        s2 = vf * vf if s2 is None else s2 + vf * vf
    mean = jnp.sum(s1, axis=-1, keepdims=True) / width
    var = jnp.sum(s2, axis=-1, keepdims=True) / width - mean * mean
    rs = lax.rsqrt(var + EPS)
    ri = lax.broadcasted_iota(jnp.int32, (L, L), 0)
    ci = lax.broadcasted_iota(jnp.int32, (L, L), 1)
    causal = ci <= ri
    reps = L // lg_ref.shape[0]
    for g, sl in enumerate(groups):
        vh = ((v_ref[:, sl].astype(F32) - mean) * rs).astype(BF16)
        vn = (vh * jnp.concatenate([lg_ref[:, sl]] * reps, axis=0)
              + jnp.concatenate([lb_ref[:, sl]] * reps, axis=0))
        wc = jnp.where(causal, ws_ref[g], 0.0).astype(BF16)
        sv = _dot(wc, vn) + bst_ref[:, g:g + 1]
        o_ref[:, sl] = u_ref[:, sl] * sv.astype(BF16)


def _gmlp_gate(u, v, ln_g, ln_b, ws, bs):
    s_len, width = u.shape
    nc = s_len // CHUNK
    row = lambda t: jnp.broadcast_to(t.astype(BF16)[None, :], (BF16_SUBLANES, width))
    return pl.pallas_call(
        _gate_kernel,
        grid=(nc,),
        in_specs=[pl.BlockSpec((CHUNK, width), lambda c: (c, 0)),
                  pl.BlockSpec((CHUNK, width), lambda c: (c, 0)),
                  pl.BlockSpec((BF16_SUBLANES, width), lambda c: (0, 0)),
                  pl.BlockSpec((BF16_SUBLANES, width), lambda c: (0, 0)),
                  pl.BlockSpec((GM_GROUPS, CHUNK, CHUNK), lambda c: (0, 0, 0)),
                  pl.BlockSpec((CHUNK, GM_GROUPS), lambda c: (0, 0))],
        out_specs=pl.BlockSpec((CHUNK, width), lambda c: (c, 0)),
        out_shape=jax.ShapeDtypeStruct((s_len, width), BF16),
        compiler_params=_cparams(1),
        name="gmlp_gate",
    )(u, v, row(ln_g), row(ln_b), ws.astype(F32), bs.T.astype(F32))


def kernel(x, ffn_norm, ffn_w13, ffn_w2, mix_norm, hy_w_in, attn_q_norm, attn_k_norm, attn_sinks,
           ssm_conv_w, ssm_conv_b, ssm_dt_bias, ssm_A_log, ssm_D, ssm_norm, hy_w_out,
           gm_w_in, gm_ln_g, gm_ln_b, gm_ws, gm_bs, gm_w_out):
    bsz, s_len, d = x.shape
    depth = ffn_norm.shape[0]
    n_attn_heads = attn_sinks.shape[1]
    n_ssm_heads = ssm_D.shape[1]
    q_width = n_attn_heads * HEAD_DIM
    kv_width = (n_attn_heads // KV_GROUP) * HEAD_DIM
    hp = n_ssm_heads * HEAD_DIM
    bc = SSM_GROUPS * SSM_STATE
    assert bsz == 1 and s_len % CHUNK == 0

    bn = COL_TILE
    nq, nkv, nzx = q_width // bn, 2 * kv_width // bn, 2 * hp // bn
    assert q_width % bn == 0 and (2 * kv_width) % bn == 0 and hp % bn == 0 and bc % bn == 0

    def proj_dest(j):
        return j + jnp.where(j < nq, 0, jnp.where(j < nq + nkv, nzx, jnp.where(j < nq + nkv + nzx, -nkv, 0)))

    n_proj = q_width + 2 * kv_width + 2 * hp + 2 * bc
    p_z, p_x = q_width, q_width + hp
    p_k = p_x + hp
    p_v = p_k + kv_width
    p_b = p_v + kv_width
    p_c = p_b + bc

    w_in_t = jnp.swapaxes(hy_w_in, 1, 2)

    xm = x.reshape(s_len, d)
    hb, rstd = _prep(xm, ffn_norm[0, 0])

    for l in range(depth):
        i = l // 2
        g_act = _pair_proj(hb, rstd, ffn_w13, (l, 0), act="swiglu", name="ffn_up")
        xm, hb, rstd = _residual_proj([g_act], ffn_w2, (l, 0), xm, mix_norm[l], alpha=0.5)
        if l % 2 == 0:
            proj = _mixer_in(hb, rstd, w_in_t, (i,), n_cols=n_proj, dest=proj_dest)
            dt_raw = _dt_proj(hb, rstd, w_in_t, (i,), dt_col=n_proj, n_heads=n_ssm_heads)
            attn = _attention(proj, attn_q_norm[i], attn_k_norm[i], attn_sinks[i],
                              q_width=q_width, k_col=p_k, v_col=p_v, kv_width=kv_width)
            ssm = _ssd(proj, dt_raw, ssm_conv_w[i], ssm_conv_b[i], ssm_dt_bias[i], ssm_A_log[i],
                       ssm_D[i], ssm_norm[i], z_col=p_z, xs_col=p_x, b_col=p_b, c_col=p_c, hp=hp)
            xm, hb, rstd = _residual_proj([attn, ssm], hy_w_out, (i,), xm, ffn_norm[l, 1], alpha=1.0)
        else:
            u, v = _pair_proj(hb, rstd, gm_w_in, (i,), act="gelu", name="gmlp_in")
            a = _gmlp_gate(u, v, gm_ln_g[i], gm_ln_b[i], gm_ws[i], gm_bs[i])
            xm, hb, rstd = _residual_proj([a], gm_w_out, (i,), xm, ffn_norm[l, 1], alpha=1.0)
        g_act = _pair_proj(hb, rstd, ffn_w13, (l, 1), act="swiglu", name="ffn_up")
        nxt = ffn_norm[l + 1, 0] if l + 1 < depth else None
        xm, hb, rstd = _residual_proj([g_act], ffn_w2, (l, 1), xm, nxt, alpha=0.5)
    return xm.reshape(bsz, s_len, d)
```
